```python
import math
import jax, jax.numpy as jnp
from jax import lax
import numpy as np

D_MODEL = 1024
BATCH = 2
SEQ = 8192
DEPTH = 1
DEC_BATCH = 128
DEC_SEQ = 1
PAST_LEN = 8192
PAGE_SIZE = 128

DIL_GROUPS = ((128, 1), (512, 4), (2048, 16))
N_GROUPS = len(DIL_GROUPS)
HEADS_PER_GROUP = 4
ATT_HEADS = N_GROUPS * HEADS_PER_GROUP
ATT_HEAD_DIM = 64
ATT_WIDTH = ATT_HEADS * ATT_HEAD_DIM
ATT_OUT_WIDTH = HEADS_PER_GROUP * ATT_HEAD_DIM
ALIBI_EXP = 8.0

DN_HEADS = 4
DN_HEAD_DIM = 128
DN_WIDTH = DN_HEADS * DN_HEAD_DIM
DN_CONV = 4
DN_CHUNK = 64
DT_MIN = 1e-3
DT_MAX = 1e-1

D_FF = 2816
FFN_CONV = 3

DEEPNORM_ALPHA = (2 * DEPTH) ** 0.25
DEEPNORM_BETA = (8 * DEPTH) ** -0.25
LN_EPS = 1e-5
RMS_EPS = 1e-6

OFF_DN_QKV = 3 * ATT_WIDTH
OFF_DN_Z = OFF_DN_QKV + 3 * DN_WIDTH
OFF_DN_BETA = OFF_DN_Z + DN_WIDTH
OFF_DN_A = OFF_DN_BETA + DN_HEADS
OFF_GATE = OFF_DN_A + DN_HEADS
IN_WIDTH = OFF_GATE + 2 * D_MODEL

kernel_name = 'hybrid_dilated_attn_gated_deltanet_convffn_deepnorm_step'


def _layer_norm(x, g, b):
    xf = x.astype(jnp.float32)
    mu = jnp.mean(xf, axis=-1, keepdims=True)
    var = jnp.mean(jnp.square(xf - mu), axis=-1, keepdims=True)
    return ((xf - mu) * lax.rsqrt(var + LN_EPS) * g.astype(jnp.float32) + b.astype(jnp.float32)).astype(x.dtype)


def _l2norm(x):
    return x * lax.rsqrt(jnp.sum(jnp.square(x), axis=-1, keepdims=True) + RMS_EPS)


def _causal_dwconv(x, w, prev):
    width, t = w.shape[0], x.shape[1]
    xp = jnp.concatenate([prev.astype(x.dtype), x], axis=1)
    out = w[0] * xp[:, :t]
    for j in range(1, width):
        out = out + w[j] * xp[:, j:j + t]
    return out, xp[:, t:]


def _alibi_slopes():
    h = jnp.arange(1, ATT_HEADS + 1, dtype=jnp.float32)
    return jnp.exp2(-ALIBI_EXP * h / ATT_HEADS).reshape(N_GROUPS, HEADS_PER_GROUP)


def _dilated_prompt(q, k, v, window, dilation, slopes):
    b, t, h, e = q.shape
    steps = window // dilation
    n_sub = t // dilation
    n_blk = -(-n_sub // steps)
    pad = n_blk * steps - n_sub

    def to_blocks(a):
        a = a.reshape(b, n_sub, dilation, h, e).transpose(0, 2, 1, 3, 4)
        a = jnp.pad(a, ((0, 0), (0, 0), (0, pad), (0, 0), (0, 0)))
        return a.reshape(b, dilation, n_blk, steps, h, e)

    def with_prev(a):
        prev = jnp.pad(a[:, :, :-1], ((0, 0), (0, 0), (1, 0), (0, 0), (0, 0), (0, 0)))
        return jnp.concatenate([prev, a], axis=3)

    qb = to_blocks(q)
    kb = with_prev(to_blocks(k))
    vb = with_prev(to_blocks(v))
    s = jnp.einsum('brnqhe,brnkhe->brnhqk', qb, kb, preferred_element_type=jnp.float32) * (e ** -0.5)
    qi = jnp.arange(steps)[:, None]
    kj = jnp.arange(2 * steps)[None, :]
    dist = steps + qi - kj
    blk = jnp.arange(n_blk)[:, None, None]
    valid = (dist >= 0) & (dist <= steps) & ((blk > 0) | (kj >= steps))
    bias = -slopes[:, None, None] * (dilation * dist).astype(jnp.float32)
    s = jnp.where(valid[:, None], s + bias, -jnp.inf)
    m = jnp.max(s, axis=-1, keepdims=True)
    p = jnp.exp(s - m)
    l = jnp.sum(p, axis=-1, keepdims=True)
    o = jnp.einsum('brnhqk,brnkhe->brnqhe', (p / l).astype(v.dtype), vb, preferred_element_type=jnp.float32)
    lse = (m + jnp.log(l))[..., 0].transpose(0, 1, 2, 4, 3)

    def from_blocks(a):
        a = a.reshape(b, dilation, n_blk * steps, *a.shape[4:])[:, :, :n_sub]
        return jnp.swapaxes(a, 1, 2).reshape(b, t, *a.shape[3:])

    return from_blocks(o), from_blocks(lse)


def _dilated_step(q, k_ext, v_ext, n_past, window, dilation, slopes):
    b, s_new, h, e = q.shape
    steps = window // dilation
    back = jnp.arange(steps + 1)
    idx = n_past + jnp.arange(s_new)[:, None] - dilation * back[None, :]
    valid = idx >= 0
    idx = jnp.maximum(idx, 0)
    kg = jnp.take(k_ext, idx, axis=1)
    vg = jnp.take(v_ext, idx, axis=1)
    sc = jnp.einsum('bshe,bsihe->bshi', q, kg, preferred_element_type=jnp.float32) * (e ** -0.5)
    bias = -slopes[:, None] * (dilation * back).astype(jnp.float32)
    sc = jnp.where(valid[:, None, :], sc + bias, -jnp.inf)
    m = jnp.max(sc, axis=-1, keepdims=True)
    p = jnp.exp(sc - m)
    l = jnp.sum(p, axis=-1, keepdims=True)
    o = jnp.einsum('bshi,bsihe->bshe', (p / l).astype(v_ext.dtype), vg, preferred_element_type=jnp.float32)
    return o, (m + jnp.log(l))[..., 0]


def _combine_groups(outs, lses):
    w = jax.nn.softmax(jnp.stack(lses, axis=0), axis=0)
    return jnp.sum(w[..., None] * jnp.stack(outs, axis=0), axis=0)


def _delta_chunked(q, k, v, beta, g):
    b, t, h, dk = q.shape
    dv = v.shape[-1]
    c = DN_CHUNK
    nc = t // c

    def chunks(a):
        return a.reshape(b, nc, c, h, a.shape[-1]).transpose(1, 0, 3, 2, 4)

    qc, kc, vc = chunks(q), chunks(k), chunks(v)
    bc = chunks(beta[..., None])
    gcum = jnp.cumsum(chunks(g[..., None])[..., 0], axis=-1)
    tri = jnp.tril(jnp.ones((c, c), dtype=bool))
    tri_strict = jnp.tril(jnp.ones((c, c), dtype=bool), -1)
    decay = jnp.exp(jnp.where(tri, gcum[..., :, None] - gcum[..., None, :], -jnp.inf))
    kk = jnp.einsum('nbhid,nbhjd->nbhij', kc * bc, kc)
    lower = jnp.where(tri_strict, kk * decay, 0.0) + jnp.eye(c, dtype=jnp.float32)
    rhs = jnp.concatenate([vc * bc, kc * bc * jnp.exp(gcum)[..., None]], axis=-1)
    sol = lax.linalg.triangular_solve(lower, rhs, left_side=True, lower=True, unit_diagonal=True)
    u0, wk = sol[..., :dv], sol[..., dv:]
    attn = jnp.einsum('nbhid,nbhjd->nbhij', qc, kc) * decay
    q_dec = qc * jnp.exp(gcum)[..., None]
    k_dec = kc * jnp.exp(gcum[..., -1:] - gcum)[..., None]
    g_last = jnp.exp(gcum[..., -1])

    def step(s, xs):
        u0_, wk_, attn_, qd_, kd_, gl_ = xs
        u = u0_ - jnp.einsum('bhid,bhde->bhie', wk_, s)
        o = jnp.einsum('bhid,bhde->bhie', qd_, s) + jnp.einsum('bhij,bhje->bhie', attn_, u)
        s = s * gl_[..., None, None] + jnp.einsum('bhid,bhie->bhde', kd_, u)
        return s, o

    s0 = jnp.zeros((b, h, dk, dv), jnp.float32)
    s_fin, o = lax.scan(step, s0, (u0, wk, attn, q_dec, k_dec, g_last))
    return o.transpose(1, 0, 3, 2, 4).reshape(b, t, h, dv), s_fin


def _delta_recurrent(q, k, v, beta, g, s0):
    def step(s, xs):
        q_, k_, v_, b_, g_ = xs
        s = s * jnp.exp(g_)[..., None, None]
        u = b_[..., None] * (v_ - jnp.einsum('bhd,bhde->bhe', k_, s))
        s = s + jnp.einsum('bhd,bhe->bhde', k_, u)
        return s, jnp.einsum('bhd,bhde->bhe', q_, s)

    xs = (jnp.swapaxes(q, 0, 1), jnp.swapaxes(k, 0, 1), jnp.swapaxes(v, 0, 1),
          jnp.swapaxes(beta, 0, 1), jnp.swapaxes(g, 0, 1))
    s_fin, o = lax.scan(step, s0, xs)
    return jnp.swapaxes(o, 0, 1), s_fin


def _deltanet(qkv_pre, z, b_raw, a_raw, conv_prev, s_prev, w_conv, a_log, dt_bias, w_norm):
    bsz, t = qkv_pre.shape[:2]
    qkv, conv_new = _causal_dwconv(qkv_pre, w_conv, conv_prev)
    qkv = jax.nn.silu(qkv.astype(jnp.float32)).reshape(bsz, t, 3, DN_HEADS, DN_HEAD_DIM)
    q = _l2norm(qkv[:, :, 0]) * (DN_HEAD_DIM ** -0.5)
    k = _l2norm(qkv[:, :, 1])
    v = qkv[:, :, 2]
    beta = jax.nn.sigmoid(b_raw.astype(jnp.float32))
    g = -jnp.exp(a_log.astype(jnp.float32)) * jax.nn.softplus(a_raw.astype(jnp.float32) + dt_bias.astype(jnp.float32))
    if s_prev is None:
        o, s_new = _delta_chunked(q, k, v, beta, g)
    else:
        o, s_new = _delta_recurrent(q, k, v, beta, g, s_prev.astype(jnp.float32))
    zf = z.astype(jnp.float32).reshape(bsz, t, DN_HEADS, DN_HEAD_DIM)
    o = o * lax.rsqrt(jnp.mean(jnp.square(o), axis=-1, keepdims=True) + RMS_EPS) * w_norm.astype(jnp.float32) * jax.nn.silu(zf)
    return o.reshape(bsz, t, DN_WIDTH).astype(qkv_pre.dtype), conv_new, s_new.astype(qkv_pre.dtype)


def _conv_ffn(h, prev, w_up, w_conv, b_conv, w_down):
    up = h @ w_up
    a, gate = up[..., :D_FF], up[..., D_FF:]
    a_c, buf_new = _causal_dwconv(a, w_conv, prev)
    return (jax.nn.gelu(a_c + b_conv) * gate) @ w_down, buf_new


def _layer(x, kv_prev, dn_conv_prev, dn_s_prev, ffn_prev, lp):
    b, t, _ = x.shape
    prompt = kv_prev is None
    if prompt:
        dn_conv_prev = jnp.zeros((b, DN_CONV - 1, 3 * DN_WIDTH), x.dtype)
        ffn_prev = jnp.zeros((b, FFN_CONV - 1, D_FF), x.dtype)
    proj = x @ lp['w_in']

    qkv_a = proj[..., :OFF_DN_QKV].reshape(b, t, 3, N_GROUPS, HEADS_PER_GROUP, ATT_HEAD_DIM)
    slopes = _alibi_slopes()
    outs, lses, kv_new = [], [], []
    for gi, (window, dil) in enumerate(DIL_GROUPS):
        q, k, v = qkv_a[:, :, 0, gi], qkv_a[:, :, 1, gi], qkv_a[:, :, 2, gi]
        kv = jnp.stack([k, v], axis=2)
        if prompt:
            o, lse = _dilated_prompt(q, k, v, window, dil, slopes[gi])
            keep = min(window, t)
            kv_new.append(kv[:, t - keep:])
        else:
            n_past = kv_prev[gi].shape[1]
            kv_ext = jnp.concatenate([kv_prev[gi].astype(x.dtype), kv], axis=1)
            o, lse = _dilated_step(q, kv_ext[:, :, 0], kv_ext[:, :, 1], n_past, window, dil, slopes[gi])
            kv_new.append(kv_ext[:, -n_past:])
        outs.append(o)
        lses.append(lse)
    o_att = _combine_groups(outs, lses).reshape(b, t, ATT_OUT_WIDTH).astype(x.dtype)

    o_dn, dn_conv_new, dn_s_new = _deltanet(
        proj[..., OFF_DN_QKV:OFF_DN_Z], proj[..., OFF_DN_Z:OFF_DN_BETA],
        proj[..., OFF_DN_BETA:OFF_DN_A], proj[..., OFF_DN_A:OFF_GATE],
        dn_conv_prev, dn_s_prev, lp['w_dn_conv'], lp['dn_a_log'], lp['dn_dt_bias'], lp['dn_onorm_w'])

    gate = jax.nn.sigmoid(proj[..., OFF_GATE:].astype(jnp.float32))
    merged = gate[..., :D_MODEL] * (o_att @ lp['w_att_out']) + gate[..., D_MODEL:] * (o_dn @ lp['w_dn_out'])
    mix = merged.astype(x.dtype) @ lp['w_o']
    h = _layer_norm(DEEPNORM_ALPHA * x + mix, lp['ln1_g'], lp['ln1_b'])

    f, ffn_new = _conv_ffn(h, ffn_prev, lp['w_up'], lp['w_ffn_conv'], lp['b_ffn_conv'], lp['w_down'])
    y = _layer_norm(DEEPNORM_ALPHA * h + f, lp['ln2_g'], lp['ln2_b'])
    return y, (kv_new[0], kv_new[1], kv_new[2], dn_conv_new, dn_s_new, ffn_new)


def setup_inputs(seed: int = 0) -> dict:
    key = jax.random.key(seed)
    ks = jax.random.split(key, 24)
    f32 = jnp.float32

    def nrm(k, shape, scale):
        return jax.random.normal(k, shape, f32) * scale

    lw = [min(w, PAST_LEN) for w, _ in DIL_GROUPS]

    def kv_shape(n):
        return (DEPTH, DEC_BATCH, n, 2, HEADS_PER_GROUP, ATT_HEAD_DIM)

    dt = jnp.exp(jax.random.uniform(ks[11], (DEPTH, DN_HEADS), f32, math.log(DT_MIN), math.log(DT_MAX)))
    return {
        'x_prompt': nrm(ks[0], (BATCH, SEQ, D_MODEL), 1.0),
        'x_sample': nrm(ks[1], (DEC_BATCH, DEC_SEQ, D_MODEL), 1.0),
        'cache_kv_w128': nrm(ks[2], kv_shape(lw[0]), 1.0),
        'cache_kv_w512': nrm(ks[3], kv_shape(lw[1]), 1.0),
        'cache_kv_w2048': nrm(ks[4], kv_shape(lw[2]), 1.0),
        'state_dn_conv': nrm(ks[5], (DEPTH, DEC_BATCH, DN_CONV - 1, 3 * DN_WIDTH), 1.0),
        'state_dn_S': nrm(ks[6], (DEPTH, DEC_BATCH, DN_HEADS, DN_HEAD_DIM, DN_HEAD_DIM), 0.3),
        'state_ffn_conv': nrm(ks[7], (DEPTH, DEC_BATCH, FFN_CONV - 1, D_FF), 1.0),
        'w_in': nrm(ks[8], (DEPTH, D_MODEL, IN_WIDTH), D_MODEL ** -0.5),
        'w_dn_conv': nrm(ks[9], (DEPTH, DN_CONV, 3 * DN_WIDTH), DN_CONV ** -0.5),
        'dn_a_log': jnp.log(jax.random.uniform(ks[10], (DEPTH, DN_HEADS), f32, 1.0, 16.0)),
        'dn_dt_bias': dt + jnp.log(-jnp.expm1(-dt)),
        'dn_onorm_w': 1.0 + nrm(ks[12], (DEPTH, DN_HEAD_DIM), 0.02),
        'w_att_out': nrm(ks[13], (DEPTH, ATT_OUT_WIDTH, D_MODEL), ATT_OUT_WIDTH ** -0.5 * DEEPNORM_BETA),
        'w_dn_out': nrm(ks[14], (DEPTH, DN_WIDTH, D_MODEL), DN_WIDTH ** -0.5 * DEEPNORM_BETA),
        'w_o': nrm(ks[15], (DEPTH, D_MODEL, D_MODEL), D_MODEL ** -0.5 * DEEPNORM_BETA),
        'ln1_g': 1.0 + nrm(ks[16], (DEPTH, D_MODEL), 0.02),
        'ln1_b': nrm(ks[17], (DEPTH, D_MODEL), 0.02),
        'w_up': nrm(ks[18], (DEPTH, D_MODEL, 2 * D_FF), D_MODEL ** -0.5),
        'w_ffn_conv': nrm(ks[19], (DEPTH, FFN_CONV, D_FF), FFN_CONV ** -0.5),
        'b_ffn_conv': nrm(ks[20], (DEPTH, D_FF), 0.02),
        'w_down': nrm(ks[21], (DEPTH, D_FF, D_MODEL), D_FF ** -0.5 * DEEPNORM_BETA),
        'ln2_g': 1.0 + nrm(ks[22], (DEPTH, D_MODEL), 0.02),
        'ln2_b': nrm(ks[23], (DEPTH, D_MODEL), 0.02),
    }


def reference(x_prompt, x_sample, cache_kv_w128, cache_kv_w512, cache_kv_w2048, state_dn_conv,
              state_dn_S, state_ffn_conv, w_in, w_dn_conv, dn_a_log, dn_dt_bias, dn_onorm_w,
              w_att_out, w_dn_out, w_o, ln1_g, ln1_b, w_up, w_ffn_conv, b_ffn_conv, w_down,
              ln2_g, ln2_b):
    def layer_params(l):
        return {'w_in': w_in[l], 'w_dn_conv': w_dn_conv[l], 'dn_a_log': dn_a_log[l],
                'dn_dt_bias': dn_dt_bias[l], 'dn_onorm_w': dn_onorm_w[l], 'w_att_out': w_att_out[l],
                'w_dn_out': w_dn_out[l], 'w_o': w_o[l], 'ln1_g': ln1_g[l], 'ln1_b': ln1_b[l],
                'w_up': w_up[l], 'w_ffn_conv': w_ffn_conv[l], 'b_ffn_conv': b_ffn_conv[l],
                'w_down': w_down[l], 'ln2_g': ln2_g[l], 'ln2_b': ln2_b[l]}

    yp, ys = x_prompt, x_sample
    p_states, s_states = [], []
    for l in range(DEPTH):
        lp = layer_params(l)
        yp, st_p = _layer(yp, None, None, None, None, lp)
        p_states.append(st_p)
        ys, st_s = _layer(ys, (cache_kv_w128[l], cache_kv_w512[l], cache_kv_w2048[l]),
                          state_dn_conv[l], state_dn_S[l], state_ffn_conv[l], lp)
        s_states.append(st_s)

    def stacked(states, i):
        return jnp.stack([st[i] for st in states], axis=0)

    return (yp, ys,
            stacked(p_states, 0), stacked(p_states, 1), stacked(p_states, 2),
            stacked(p_states, 3), stacked(p_states, 4), stacked(p_states, 5),
            stacked(s_states, 0), stacked(s_states, 1), stacked(s_states, 2),
            stacked(s_states, 3), stacked(s_states, 4), stacked(s_states, 5))
```

```python
import functools
import math

import jax
import jax.numpy as jnp
from jax import lax
from jax.experimental import pallas as pl
from jax.experimental.pallas import tpu as pltpu

F32 = jnp.float32
BF16 = jnp.bfloat16

D_MODEL = 1024
DIL_GROUPS = ((128, 1), (512, 4), (2048, 16))
N_GROUPS = len(DIL_GROUPS)
HEADS = 4
HEAD_DIM = 64
GROUP_W = HEADS * HEAD_DIM
ATT_W = N_GROUPS * GROUP_W
ATT_QKV_W = 3 * ATT_W
ALIBI_EXP = 8.0
ATT_STEPS = 128
DN_HEADS = 4
DN_DIM = 128
DN_W = DN_HEADS * DN_DIM
DN_QKV_W = 3 * DN_W
DN_CONV = 4
DN_CHUNK = 64
D_FF = 2816
FFN_CONV = 3
LN_EPS = 1e-5
RMS_EPS = 1e-6
OFF_DN_QKV = ATT_QKV_W
OFF_DN_Z = OFF_DN_QKV + DN_QKV_W
OFF_DN_BETA = OFF_DN_Z + DN_W
OFF_GATE = OFF_DN_BETA + 2 * DN_HEADS
W_MAIN = OFF_DN_BETA + 128
NEG_BIG = -1e30

V7X_VMEM_LIMIT_BYTES = 56 * 1024 * 1024
LANES = 128
SUBLANES = 8


def _cparams(*sem):
    return pltpu.CompilerParams(dimension_semantics=sem, vmem_limit_bytes=V7X_VMEM_LIMIT_BYTES)


def _resident(shape):
    nd = len(shape)
    return pl.BlockSpec(shape, lambda *_: (0,) * nd, pipeline_mode=pl.Buffered(1))


def _bdot(a, b):
    return jnp.dot(a.astype(BF16), b.astype(BF16), preferred_element_type=F32)


def _bdot_nt(a, b):
    return lax.dot_general(a.astype(BF16), b.astype(BF16), (((1,), (1,)), ((), ())),
                           preferred_element_type=F32)


def _sigmoid(x):
    return 1.0 / (1.0 + jnp.exp(-x))


def _silu(x):
    return x * _sigmoid(x)


def _softplus(x):
    return jnp.maximum(x, 0.0) + jnp.log(1.0 + jnp.exp(-jnp.abs(x)))


def _gelu_tanh(x):
    return 0.5 * x * (1.0 + jnp.tanh(math.sqrt(2.0 / math.pi) * (x + 0.044715 * (x * x * x))))


def _layer_norm(r, g, b):
    mu = jnp.mean(r, axis=-1, keepdims=True)
    d = r - mu
    var = jnp.mean(d * d, axis=-1, keepdims=True)
    return d * lax.rsqrt(var + LN_EPS) * g + b


def _proj_kernel(x_ref, wm_ref, wg_ref, att_ref, dn_ref, z_ref, ba_ref, gate_ref):
    xb = x_ref[...].astype(BF16)

    def mm(w_ref, lo, width):
        return jnp.dot(xb, w_ref[:, lo:lo + width], preferred_element_type=F32)

    for lo in range(0, ATT_QKV_W, ATT_W):
        att_ref[:, lo:lo + ATT_W] = mm(wm_ref, lo, ATT_W).astype(att_ref.dtype)
    for lo in range(0, DN_QKV_W, DN_W):
        dn_ref[:, lo:lo + DN_W] = mm(wm_ref, OFF_DN_QKV + lo, DN_W).astype(dn_ref.dtype)
    z_ref[...] = mm(wm_ref, OFF_DN_Z, DN_W).astype(z_ref.dtype)
    ba_ref[...] = mm(wm_ref, OFF_DN_BETA, LANES)
    for lo in range(0, 2 * D_MODEL, 512):
        gate_ref[:, lo:lo + 512] = mm(wg_ref, lo, 512).astype(gate_ref.dtype)


def _proj(x2d, wm, wg, tm, act_dtype):
    m = x2d.shape[0]
    row = lambda w: pl.BlockSpec((tm, w), lambda i: (i, 0))
    return pl.pallas_call(
        _proj_kernel,
        grid=(m // tm,),
        in_specs=[row(D_MODEL), _resident(wm.shape), _resident(wg.shape)],
        out_specs=[row(ATT_QKV_W), row(DN_QKV_W), row(DN_W), row(LANES), row(2 * D_MODEL)],
        out_shape=[jax.ShapeDtypeStruct((m, ATT_QKV_W), act_dtype),
                   jax.ShapeDtypeStruct((m, DN_QKV_W), act_dtype),
                   jax.ShapeDtypeStruct((m, DN_W), act_dtype),
                   jax.ShapeDtypeStruct((m, LANES), F32),
                   jax.ShapeDtypeStruct((m, 2 * D_MODEL), act_dtype)],
        compiler_params=_cparams("arbitrary"),
        name="in_proj",
    )(x2d, wm, wg)


def _alibi_slopes(group):
    return tuple(2.0 ** (-ALIBI_EXP * (group * HEADS + h + 1) / (N_GROUPS * HEADS)) for h in range(HEADS))


def _attn_kernel(q_ref, kc_ref, kp_ref, vc_ref, vp_ref, o_ref, lse_ref, kbuf, vbuf, *, dil, slopes, nsub):
    sb = ATT_STEPS
    n = pl.program_id(2)
    kbuf[0:sb, :] = kp_ref[...]
    kbuf[sb:, :] = kc_ref[...]
    vbuf[0:sb, :] = vp_ref[...]
    vbuf[sb:, :] = vc_ref[...]

    qi = lax.broadcasted_iota(jnp.int32, (sb, 2 * sb), 0)
    kj = lax.broadcasted_iota(jnp.int32, (sb, 2 * sb), 1)
    dist = sb + qi - kj
    in_band = (dist >= 0) & (dist <= sb)
    first_ok = in_band & (kj >= jnp.where(n == 0, sb, 0))
    distf = (dil * dist).astype(F32)
    bias_rest = [jnp.where(in_band, -slopes[h] * distf, NEG_BIG) for h in range(HEADS)]
    bias_first = [jnp.where(first_ok, -slopes[h] * distf, NEG_BIG) for h in range(HEADS)]
    lane_head = lax.broadcasted_iota(jnp.int32, (sb, GROUP_W), 1) // HEAD_DIM

    for j in range(nsub):
        q = q_ref[j * sb:(j + 1) * sb, :].astype(F32) * (HEAD_DIM ** -0.5)
        kk = kbuf[j * sb:(j + 2) * sb, :]
        vv = vbuf[j * sb:(j + 2) * sb, :]
        qs = jnp.concatenate([jnp.where(lane_head == h, q, 0.0) for h in range(HEADS)], axis=0)
        s = _bdot_nt(qs, kk)
        bias = bias_first if j == 0 else bias_rest
        probs, lses = [], []
        for h in range(HEADS):
            sh = s[h * sb:(h + 1) * sb, :] + bias[h]
            m = jnp.max(sh, axis=-1, keepdims=True)
            p = jnp.exp(sh - m)
            l = jnp.sum(p, axis=-1, keepdims=True)
            probs.append((p * (1.0 / l)).astype(BF16))
            lses.append(m + jnp.log(l))
        pv = jnp.dot(jnp.concatenate(probs, axis=0), vv, preferred_element_type=F32)
        o = jnp.zeros((sb, GROUP_W), F32)
        lse = jnp.zeros((sb, GROUP_W), F32)
        for h in range(HEADS):
            o = jnp.where(lane_head == h, pv[h * sb:(h + 1) * sb, :], o)
            lse = jnp.where(lane_head == h, lses[h], lse)
        o_ref[j * sb:(j + 1) * sb, :] = o.astype(o_ref.dtype)
        lse_ref[j * sb:(j + 1) * sb, :] = lse


def _attn_prompt_group(att, group, bsz, t):
    _, dil = DIL_GROUPS[group]
    sb = ATT_STEPS
    n_sub_rows = t // dil
    nsub = min(4, n_sub_rows // sb)
    qb = nsub * sb
    assert n_sub_rows % qb == 0
    nblk_cols = ATT_QKV_W // GROUP_W
    a3 = att.reshape(bsz, n_sub_rows, dil * ATT_QKV_W)

    def cur(sect):
        return pl.BlockSpec((None, qb, GROUP_W), lambda b, r, n: (b, n, r * nblk_cols + sect * N_GROUPS + group))

    def prev(sect):
        return pl.BlockSpec((None, sb, GROUP_W),
                            lambda b, r, n: (b, jnp.maximum(n * nsub - 1, 0), r * nblk_cols + sect * N_GROUPS + group))

    out_spec = pl.BlockSpec((None, qb, GROUP_W), lambda b, r, n: (b, n, r))
    o, lse = pl.pallas_call(
        functools.partial(_attn_kernel, dil=dil, slopes=_alibi_slopes(group), nsub=nsub),
        grid=(bsz, dil, n_sub_rows // qb),
        in_specs=[cur(0), cur(1), prev(1), cur(2), prev(2)],
        out_specs=[out_spec, out_spec],
        out_shape=[jax.ShapeDtypeStruct((bsz, n_sub_rows, dil * GROUP_W), BF16),
                   jax.ShapeDtypeStruct((bsz, n_sub_rows, dil * GROUP_W), F32)],
        scratch_shapes=[pltpu.VMEM((qb + sb, GROUP_W), att.dtype), pltpu.VMEM((qb + sb, GROUP_W), att.dtype)],
        compiler_params=_cparams("arbitrary", "arbitrary", "arbitrary"),
        name=f"attn_prompt_g{group}",
    )(a3, a3, a3, a3, a3)
    return o.reshape(bsz * t, GROUP_W), lse.reshape(bsz * t, GROUP_W)


def _dn_gates(ba, alog, dtb):
    beta = _sigmoid(ba)
    g = -jnp.exp(alog) * _softplus(ba + dtb)
    return beta, g


def _l2norm(x):
    return x * lax.rsqrt(jnp.sum(x * x, axis=-1, keepdims=True) + RMS_EPS)


def _gated_rmsnorm(o, z, w):
    return o * lax.rsqrt(jnp.mean(o * o, axis=-1, keepdims=True) + RMS_EPS) * w * _silu(z)


def _dn_prompt_kernel(dn_ref, ba_ref, z_ref, wconv_ref, alog_ref, dtb_ref, onw_ref,
                      o_ref, cst_ref, sfin_ref, xbuf, s_scr, *, nb):
    c = pl.program_id(0)
    ch = DN_CHUNK
    halo = SUBLANES

    @pl.when(c == 0)
    def _():
        xbuf[...] = jnp.zeros_like(xbuf)
        s_scr[...] = jnp.zeros_like(s_scr)

    row = lax.broadcasted_iota(jnp.int32, (ch, ch), 0)
    col = lax.broadcasted_iota(jnp.int32, (ch, ch), 1)
    lower = row >= col
    strict = row > col
    row_l = lax.broadcasted_iota(jnp.int32, (ch, LANES), 0)
    w = wconv_ref[...]
    onw = onw_ref[...]

    for b in range(nb):
        x = dn_ref[b].astype(F32)
        xbuf[b, halo:halo + ch, :] = x
        conv = w[DN_CONV - 1:DN_CONV, :] * x
        for j in range(DN_CONV - 1):
            conv = conv + w[j:j + 1, :] * xbuf[b, halo - (DN_CONV - 1) + j:halo - (DN_CONV - 1) + j + ch, :]
        tail = xbuf[b, halo + ch - (DN_CONV - 1):halo + ch, :]
        cst_ref[b] = tail
        xbuf[b, halo - (DN_CONV - 1):halo, :] = tail
        act = _silu(conv)

        beta_all, g_all = _dn_gates(ba_ref[b], alog_ref[...], dtb_ref[...])
        gc = g_all
        for sft in (1, 2, 4, 8, 16, 32):
            gc = gc + jnp.where(row_l >= sft, pltpu.roll(gc, sft, 0), 0.0)
        gct = gc.T
        egc = jnp.exp(gc)
        glast = gc[ch - 1:ch, :]
        eglast = jnp.exp(glast)
        ekd = jnp.exp(glast - gc)

        for h in range(DN_HEADS):
            q = _l2norm(act[:, h * DN_DIM:(h + 1) * DN_DIM]) * (DN_DIM ** -0.5)
            k = _l2norm(act[:, DN_W + h * DN_DIM:DN_W + (h + 1) * DN_DIM])
            v = act[:, 2 * DN_W + h * DN_DIM:2 * DN_W + (h + 1) * DN_DIM]
            beta = beta_all[:, h:h + 1]
            ga = DN_HEADS + h
            diff = gc[:, ga:ga + 1] - gct[ga:ga + 1, :]
            decay = jnp.where(lower, jnp.exp(jnp.minimum(diff, 0.0)), 0.0)
            kb = k * beta
            eg = egc[:, ga:ga + 1]
            xm = jnp.where(strict, -(_bdot_nt(kb, k) * decay), 0.0)
            sol = jnp.concatenate([v * beta, kb * eg], axis=-1)
            sol = sol + _bdot(xm, sol)
            for _ in range(5):
                xm = _bdot(xm, xm)
                sol = sol + _bdot(xm, sol)
            u0 = sol[:, :DN_DIM]
            wk = sol[:, DN_DIM:]
            attn = _bdot_nt(q, k) * decay
            s_old = s_scr[b, h]
            u = u0 - _bdot(wk, s_old)
            o = _bdot(q * eg, s_old) + _bdot(attn, u)
            kd = k * ekd[:, ga:ga + 1]
            s_scr[b, h] = s_old * eglast[:, ga:ga + 1] + _bdot(kd.T, u)
            zf = z_ref[b, :, h * DN_DIM:(h + 1) * DN_DIM].astype(F32)
            o_ref[b, :, h * DN_DIM:(h + 1) * DN_DIM] = _gated_rmsnorm(o, zf, onw).astype(o_ref.dtype)

    @pl.when(c == pl.num_programs(0) - 1)
    def _():
        sfin_ref[...] = s_scr[...]


def _dn_prompt(dn, ba, z, wconv, alog128, dtb128, onw, bsz, t):
    ch = DN_CHUNK
    blk = lambda w: pl.BlockSpec((bsz, ch, w), lambda c: (0, c, 0))
    return pl.pallas_call(
        functools.partial(_dn_prompt_kernel, nb=bsz),
        grid=(t // ch,),
        in_specs=[blk(DN_QKV_W), blk(LANES), blk(DN_W),
                  _resident(wconv.shape), _resident(alog128.shape), _resident(dtb128.shape), _resident(onw.shape)],
        out_specs=[blk(DN_W),
                   pl.BlockSpec((bsz, DN_CONV - 1, DN_QKV_W), lambda c: (0, 0, 0)),
                   pl.BlockSpec((bsz, DN_HEADS, DN_DIM, DN_DIM), lambda c: (0, 0, 0, 0))],
        out_shape=[jax.ShapeDtypeStruct((bsz, t, DN_W), BF16),
                   jax.ShapeDtypeStruct((bsz, DN_CONV - 1, DN_QKV_W), F32),
                   jax.ShapeDtypeStruct((bsz, DN_HEADS, DN_DIM, DN_DIM), F32)],
        scratch_shapes=[pltpu.VMEM((bsz, SUBLANES + ch, DN_QKV_W), F32),
                        pltpu.VMEM((bsz, DN_HEADS, DN_DIM, DN_DIM), F32)],
        compiler_params=_cparams("arbitrary"),
        name="deltanet_prompt",
    )(dn.reshape(bsz, t, DN_QKV_W), ba.reshape(bsz, t, LANES), z.reshape(bsz, t, DN_W),
      wconv, alog128, dtb128, onw)


def _merge_kernel(*refs, n_groups, alpha):
    att_refs = refs[:2 * n_groups] if n_groups > 1 else refs[:1]
    odn_ref, gate_ref, x_ref, wa_ref, wd_ref, wo_ref, g_ref, b_ref, h_ref = refs[len(att_refs):]
    if n_groups > 1:
        lses = [att_refs[2 * i + 1][...] for i in range(n_groups)]
        m = functools.reduce(jnp.maximum, lses)
        es = [jnp.exp(l - m) for l in lses]
        num = sum(e * att_refs[2 * i][...].astype(F32) for i, e in enumerate(es))
        o_att = num * (1.0 / sum(es))
    else:
        o_att = att_refs[0][...]
    a = jnp.dot(o_att.astype(BF16), wa_ref[...], preferred_element_type=F32)
    bm = jnp.dot(odn_ref[...].astype(BF16), wd_ref[...], preferred_element_type=F32)
    sg = _sigmoid(gate_ref[...].astype(F32))
    merged = sg[:, :D_MODEL] * a + sg[:, D_MODEL:] * bm
    mix = jnp.dot(merged.astype(BF16), wo_ref[...], preferred_element_type=F32)
    h_ref[...] = _layer_norm(alpha * x_ref[...] + mix, g_ref[...], b_ref[...])


def _merge(att_parts, odn, gates, x2d, wa, wd, wo, g, b, tm, alpha):
    m = x2d.shape[0]
    n_groups = len(att_parts) // 2 if len(att_parts) > 1 else 1
    row = lambda w: pl.BlockSpec((tm, w), lambda i: (i, 0))
    return pl.pallas_call(
        functools.partial(_merge_kernel, n_groups=n_groups, alpha=alpha),
        grid=(m // tm,),
        in_specs=[row(GROUP_W)] * len(att_parts) + [row(DN_W), row(2 * D_MODEL), row(D_MODEL),
                  _resident(wa.shape), _resident(wd.shape), _resident(wo.shape),
                  _resident(g.shape), _resident(b.shape)],
        out_specs=row(D_MODEL),
        out_shape=jax.ShapeDtypeStruct((m, D_MODEL), F32),
        compiler_params=_cparams("arbitrary"),
        name="merge_ln1",
    )(*att_parts, odn, gates, x2d, wa, wd, wo, g, b)


FF_CHUNK = D_FF // 2


def _ffn_prompt_kernel(h_ref, wup_ref, wconv_ref, bconv_ref, wdn_ref, g_ref, b_ref,
                       y_ref, cst_ref, halo, *, tm, alpha):
    @pl.when(pl.program_id(1) == 0)
    def _():
        halo[...] = jnp.zeros_like(halo)

    h = h_ref[...]
    hb = h.astype(BF16)
    row = lax.broadcasted_iota(jnp.int32, (tm, FF_CHUNK), 0)
    acc = jnp.zeros((tm, D_MODEL), F32)
    for c in range(D_FF // FF_CHUNK):
        cols = slice(c * FF_CHUNK, (c + 1) * FF_CHUNK)
        a = jnp.dot(hb, wup_ref[:, cols], preferred_element_type=F32)
        gt = jnp.dot(hb, wup_ref[:, D_FF + c * FF_CHUNK:D_FF + (c + 1) * FF_CHUNK], preferred_element_type=F32)
        prev2 = halo[0:1, cols]
        prev1 = halo[1:2, cols]
        a1 = jnp.where(row == 0, prev1, pltpu.roll(a, 1, 0))
        a2 = jnp.where(row == 0, prev2, jnp.where(row == 1, prev1, pltpu.roll(a, 2, 0)))
        wc = wconv_ref[:, cols]
        conv = wc[0:1, :] * a2 + wc[1:2, :] * a1 + wc[2:3, :] * a + bconv_ref[:, cols]
        act = _gelu_tanh(conv) * gt
        acc = acc + jnp.dot(act.astype(BF16), wdn_ref[cols, :], preferred_element_type=F32)
        tail = a[tm - (FFN_CONV - 1):tm, :]
        halo[0:FFN_CONV - 1, cols] = tail
        cst_ref[:, cols] = tail
    y_ref[...] = _layer_norm(alpha * h + acc, g_ref[...], b_ref[...])


def _ffn_prompt(h2d, wup, wconv, bconv, wdn, g, b, bsz, t, tm, alpha):
    nt = t // tm
    return pl.pallas_call(
        functools.partial(_ffn_prompt_kernel, tm=tm, alpha=alpha),
        grid=(bsz, nt),
        in_specs=[pl.BlockSpec((tm, D_MODEL), lambda bi, i: (bi * nt + i, 0)),
                  _resident(wup.shape), _resident(wconv.shape), _resident(bconv.shape), _resident(wdn.shape),
                  _resident(g.shape), _resident(b.shape)],
        out_specs=[pl.BlockSpec((tm, D_MODEL), lambda bi, i: (bi * nt + i, 0)),
                   pl.BlockSpec((None, FFN_CONV - 1, D_FF), lambda bi, i: (bi, 0, 0))],
        out_shape=[jax.ShapeDtypeStruct((bsz * t, D_MODEL), F32),
                   jax.ShapeDtypeStruct((bsz, FFN_CONV - 1, D_FF), F32)],
        scratch_shapes=[pltpu.VMEM((SUBLANES, D_FF), F32)],
        compiler_params=_cparams("arbitrary", "arbitrary"),
        name="convffn_prompt",
    )(h2d, wup, wconv, bconv, wdn, g, b)


def _ffn_step_kernel(h_ref, prev_ref, wup_ref, wconv_ref, bconv_ref, wdn_ref, g_ref, b_ref,
                     y_ref, st_ref, *, alpha):
    h = h_ref[...]
    hb = h.astype(BF16)
    a = jnp.dot(hb, wup_ref[:, :D_FF], preferred_element_type=F32)
    gt = jnp.dot(hb, wup_ref[:, D_FF:], preferred_element_type=F32)
    p0 = prev_ref[:, :D_FF]
    p1 = prev_ref[:, D_FF:]
    wc = wconv_ref[...]
    conv = wc[0:1, :] * p0 + wc[1:2, :] * p1 + wc[2:3, :] * a + bconv_ref[...]
    act = _gelu_tanh(conv) * gt
    f = jnp.dot(act.astype(BF16), wdn_ref[...], preferred_element_type=F32)
    y_ref[...] = _layer_norm(alpha * h + f, g_ref[...], b_ref[...])
    st_ref[:, :D_FF] = p1
    st_ref[:, D_FF:] = a


def _ffn_step(h2d, prev2d, wup, wconv, bconv, wdn, g, b, alpha):
    m = h2d.shape[0]
    full = lambda a: pl.BlockSpec(a.shape, lambda i: (0,) * a.ndim)
    ins = (h2d, prev2d, wup, wconv, bconv, wdn, g, b)
    return pl.pallas_call(
        functools.partial(_ffn_step_kernel, alpha=alpha),
        grid=(1,),
        in_specs=[full(a) for a in ins],
        out_specs=[pl.BlockSpec((m, D_MODEL), lambda i: (0, 0)),
                   pl.BlockSpec((m, (FFN_CONV - 1) * D_FF), lambda i: (0, 0))],
        out_shape=[jax.ShapeDtypeStruct((m, D_MODEL), F32),
                   jax.ShapeDtypeStruct((m, (FFN_CONV - 1) * D_FF), F32)],
        compiler_params=_cparams("arbitrary"),
        name="convffn_step",
    )(*ins)


def _mixer_step_kernel(att_ref, dn_ref, z_ref, ba_ref, c0_ref, c1_ref, c2_ref, cst_ref, s_ref,
                       wconv_ref, alog_ref, dtb_ref, onw_ref,
                       oatt_ref, odn_ref, n0_ref, n1_ref, n2_ref, ncst_ref, ns_ref):
    sb = ATT_STEPS
    att = att_ref[...]
    sub = lax.broadcasted_iota(jnp.int32, (SUBLANES, GROUP_W), 0)
    lane_head = lax.broadcasted_iota(jnp.int32, (SUBLANES, GROUP_W), 1) // HEAD_DIM
    head_sel = sub == lane_head
    back = (sb - lax.broadcasted_iota(jnp.int32, (SUBLANES, sb), 1)).astype(F32)
    sub_s = lax.broadcasted_iota(jnp.int32, (SUBLANES, sb), 0)

    o_parts, lse_parts = [], []
    for gi, (cache_ref, new_ref) in enumerate(((c0_ref, n0_ref), (c1_ref, n1_ref), (c2_ref, n2_ref))):
        window, dil = DIL_GROUPS[gi]
        rows = cache_ref.shape[0]
        assert rows == 4 * window and window == sb * dil
        q = att[:, gi * GROUP_W:(gi + 1) * GROUP_W] * (HEAD_DIM ** -0.5)
        k_new = att[:, ATT_W + gi * GROUP_W:ATT_W + (gi + 1) * GROUP_W]
        v_new = att[:, 2 * ATT_W + gi * GROUP_W:2 * ATT_W + (gi + 1) * GROUP_W]
        new_ref[0:rows - 4, :] = cache_ref[4:rows, :]
        new_ref[rows - 4:rows - 3, :] = k_new[:, :LANES]
        new_ref[rows - 3:rows - 2, :] = k_new[:, LANES:]
        new_ref[rows - 2:rows - 1, :] = v_new[:, :LANES]
        new_ref[rows - 1:rows, :] = v_new[:, LANES:]
        part = lambda c: cache_ref[pl.ds(c, sb, stride=4 * dil), :]
        kmat = jnp.concatenate([part(0), part(1)], axis=-1)
        vmat = jnp.concatenate([part(2), part(3)], axis=-1)
        qm = jnp.where(head_sel, jnp.broadcast_to(q, (SUBLANES, GROUP_W)), 0.0)
        slopes = _alibi_slopes(gi)
        slope_col = jnp.zeros((SUBLANES, sb), F32)
        for h in range(HEADS):
            slope_col = jnp.where(sub_s == h, slopes[h] * dil, slope_col)
        s_old = _bdot_nt(qm, kmat) - slope_col * back
        s_new = jnp.sum(qm * k_new, axis=-1, keepdims=True)
        m = jnp.maximum(jnp.max(s_old, axis=-1, keepdims=True), s_new)
        p_old = jnp.exp(s_old - m)
        p_new = jnp.exp(s_new - m)
        l = jnp.sum(p_old, axis=-1, keepdims=True) + p_new
        inv_l = 1.0 / l
        o8 = _bdot(p_old * inv_l, vmat) + (p_new * inv_l) * v_new
        o_parts.append(o8)
        lse_parts.append(m + jnp.log(l))
    mx = functools.reduce(jnp.maximum, lse_parts)
    es = [jnp.exp(l - mx) for l in lse_parts]
    comb = sum(e * o for e, o in zip(es, o_parts)) * (1.0 / sum(es))
    oatt_ref[...] = jnp.sum(jnp.where(head_sel, comb, 0.0), axis=0, keepdims=True)

    cst = cst_ref[...]
    x_new = dn_ref[...]
    w = wconv_ref[...]
    conv = w[DN_CONV - 1:DN_CONV, :] * x_new
    for j in range(DN_CONV - 1):
        conv = conv + w[j:j + 1, :] * cst[j:j + 1, :]
    ncst_ref[0:DN_CONV - 2, :] = cst[1:DN_CONV - 1, :]
    ncst_ref[DN_CONV - 2:DN_CONV - 1, :] = x_new
    act = _silu(conv)
    beta_all, g_all = _dn_gates(ba_ref[...], alog_ref[...], dtb_ref[...])
    eye = (lax.broadcasted_iota(jnp.int32, (DN_DIM, DN_DIM), 0)
           == lax.broadcasted_iota(jnp.int32, (DN_DIM, DN_DIM), 1))
    to_col = lambda r: jnp.sum(jnp.where(eye, r, 0.0), axis=-1, keepdims=True)
    for h in range(DN_HEADS):
        q = _l2norm(act[:, h * DN_DIM:(h + 1) * DN_DIM]) * (DN_DIM ** -0.5)
        k = _l2norm(act[:, DN_W + h * DN_DIM:DN_W + (h + 1) * DN_DIM])
        v = act[:, 2 * DN_W + h * DN_DIM:2 * DN_W + (h + 1) * DN_DIM]
        beta = beta_all[:, h:h + 1]
        s = s_ref[h] * jnp.exp(g_all[:, DN_HEADS + h:DN_HEADS + h + 1])
        k_col = to_col(k)
        u = beta * (v - jnp.sum(k_col * s, axis=0, keepdims=True))
        s = s + k_col * u
        ns_ref[h] = s
        o = jnp.sum(to_col(q) * s, axis=0, keepdims=True)
        odn_ref[:, h * DN_DIM:(h + 1) * DN_DIM] = _gated_rmsnorm(o, z_ref[:, h * DN_DIM:(h + 1) * DN_DIM], onw_ref[...])


def _mixer_step(att, dn, z, ba, caches, cst, s0, wconv, alog128, dtb128, onw):
    nb = att.shape[0]
    per_b = lambda *dims: pl.BlockSpec((None,) + dims, lambda b: (b,) + (0,) * len(dims))
    c3 = [c.reshape(nb, -1, LANES) for c in caches]
    ins = (att.reshape(nb, 1, -1), dn.reshape(nb, 1, -1), z.reshape(nb, 1, -1), ba.reshape(nb, 1, -1),
           *c3, cst, s0, wconv, alog128, dtb128, onw)
    in_specs = ([per_b(1, ATT_QKV_W), per_b(1, DN_QKV_W), per_b(1, DN_W), per_b(1, LANES)]
                + [per_b(c.shape[1], LANES) for c in c3]
                + [per_b(DN_CONV - 1, DN_QKV_W), per_b(DN_HEADS, DN_DIM, DN_DIM),
                   _resident(wconv.shape), _resident(alog128.shape), _resident(dtb128.shape), _resident(onw.shape)])
    out_specs = ([per_b(1, GROUP_W), per_b(1, DN_W)] + [per_b(c.shape[1], LANES) for c in c3]
                 + [per_b(DN_CONV - 1, DN_QKV_W), per_b(DN_HEADS, DN_DIM, DN_DIM)])
    out_shape = ([jax.ShapeDtypeStruct((nb, 1, GROUP_W), F32), jax.ShapeDtypeStruct((nb, 1, DN_W), F32)]
                 + [jax.ShapeDtypeStruct(c.shape, F32) for c in c3]
                 + [jax.ShapeDtypeStruct(cst.shape, F32), jax.ShapeDtypeStruct(s0.shape, F32)])
    return pl.pallas_call(
        _mixer_step_kernel,
        grid=(nb,),
        in_specs=in_specs,
        out_specs=out_specs,
        out_shape=out_shape,
        compiler_params=_cparams("arbitrary"),
        name="mixer_step",
    )(*ins)


def _pad_lanes(v, offset):
    return jnp.zeros((1, LANES), F32).at[0, offset:offset + v.shape[0]].set(v.astype(F32))


def kernel(x_prompt, x_sample, cache_kv_w128, cache_kv_w512, cache_kv_w2048, state_dn_conv, state_dn_S,
           state_ffn_conv, w_in, w_dn_conv, dn_a_log, dn_dt_bias, dn_onorm_w, w_att_out, w_dn_out, w_o,
           ln1_g, ln1_b, w_up, w_ffn_conv, b_ffn_conv, w_down, ln2_g, ln2_b):
    depth = w_in.shape[0]
    alpha = (2 * depth) ** 0.25
    bsz, t, _ = x_prompt.shape
    nb = x_sample.shape[0]
    assert x_sample.shape[1] == 1
    caches_in = (cache_kv_w128, cache_kv_w512, cache_kv_w2048)

    yp = x_prompt.reshape(bsz * t, D_MODEL)
    ys = x_sample.reshape(nb, D_MODEL)
    p_states, s_states = [], []
    for l in range(depth):
        wm = w_in[l, :, :W_MAIN].astype(BF16)
        wg = w_in[l, :, OFF_GATE:].astype(BF16)
        wa, wd, wo = w_att_out[l].astype(BF16), w_dn_out[l].astype(BF16), w_o[l].astype(BF16)
        wup, wdn = w_up[l].astype(BF16), w_down[l].astype(BF16)
        wconv = w_dn_conv[l]
        alog128 = _pad_lanes(dn_a_log[l], DN_HEADS)
        dtb128 = _pad_lanes(dn_dt_bias[l], DN_HEADS)
        onw = dn_onorm_w[l].reshape(1, DN_DIM)
        g1, b1 = ln1_g[l].reshape(1, D_MODEL), ln1_b[l].reshape(1, D_MODEL)
        g2, b2 = ln2_g[l].reshape(1, D_MODEL), ln2_b[l].reshape(1, D_MODEL)
        wfc, bfc = w_ffn_conv[l], b_ffn_conv[l].reshape(1, D_FF)

        att, dn, z, ba, gates = _proj(yp, wm, wg, 512, BF16)
        att_parts = []
        for gi in range(N_GROUPS):
            att_parts.extend(_attn_prompt_group(att, gi, bsz, t))
        odn, p_cst, p_s = _dn_prompt(dn, ba, z, wconv, alog128, dtb128, onw, bsz, t)
        hp = _merge(att_parts, odn.reshape(bsz * t, DN_W), gates, yp, wa, wd, wo, g1, b1, 512, alpha)
        yp_new, p_ffn = _ffn_prompt(hp, wup, wfc, bfc, wdn, g2, b2, bsz, t, 512, alpha)
        att3 = att.reshape(bsz, t, 3, N_GROUPS, HEADS, HEAD_DIM)
        p_kv = []
        for gi, (window, _) in enumerate(DIL_GROUPS):
            keep = min(window, t)
            p_kv.append(att3[:, t - keep:, 1:3, gi].astype(F32))
        p_states.append((*p_kv, p_cst, p_s, p_ffn))

        att_s, dn_s, z_s, ba_s, gates_s = _proj(ys, wm, wg, nb, F32)
        (oatt_s, odn_s, n0, n1, n2, n_cst, n_s) = _mixer_step(
            att_s, dn_s, z_s, ba_s, [c[l] for c in caches_in], state_dn_conv[l], state_dn_S[l],
            wconv, alog128, dtb128, onw)
        hs = _merge([oatt_s.reshape(nb, GROUP_W)], odn_s.reshape(nb, DN_W), gates_s, ys,
                    wa, wd, wo, g1, b1, nb, alpha)
        ys_new, s_ffn = _ffn_step(hs, state_ffn_conv[l].reshape(nb, (FFN_CONV - 1) * D_FF),
                                  wup, wfc, bfc, wdn, g2, b2, alpha)
        s_kv = [n.reshape(c[l].shape) for n, c in zip((n0, n1, n2), caches_in)]
        s_states.append((*s_kv, n_cst, n_s, s_ffn.reshape(nb, FFN_CONV - 1, D_FF)))
        yp, ys = yp_new, ys_new

    stacked = lambda states, i: jnp.stack([st[i] for st in states], axis=0)
    return (yp.reshape(bsz, t, D_MODEL), ys.reshape(nb, 1, D_MODEL),
            *[stacked(p_states, i) for i in range(6)],
            *[stacked(s_states, i) for i in range(6)])
```

```python
import functools
import math

import jax
import jax.numpy as jnp
from jax import lax
from jax.experimental import pallas as pl
from jax.experimental.pallas import tpu as pltpu

F32 = jnp.float32
BF16 = jnp.bfloat16

D_MODEL = 1024
DIL_GROUPS = ((128, 1), (512, 4), (2048, 16))
N_GROUPS = len(DIL_GROUPS)
HEADS = 4
HEAD_DIM = 64
GROUP_W = HEADS * HEAD_DIM
QKV_W = 3 * GROUP_W
ATT_W = N_GROUPS * GROUP_W
ATT_QKV_W = 3 * ATT_W
ALIBI_EXP = 8.0
ATT_STEPS = 128
DN_HEADS = 4
DN_DIM = 128
DN_W = DN_HEADS * DN_DIM
DN_QKV_W = 3 * DN_W
DN_CONV = 4
DN_CHUNK = 64
D_FF = 2816
FFN_CONV = 3
LN_EPS = 1e-5
RMS_EPS = 1e-6
OFF_DN_QKV = ATT_QKV_W
OFF_DN_Z = OFF_DN_QKV + DN_QKV_W
OFF_DN_BETA = OFF_DN_Z + DN_W
OFF_GATE = OFF_DN_BETA + 2 * DN_HEADS
W_MAIN = OFF_DN_BETA + 128
NEG_BIG = -1e30

V7X_VMEM_LIMIT_BYTES = 56 * 1024 * 1024
LANES = 128
SUBLANES = 8

ROW_TILE = 512
ATT_SUB_BLOCKS = 4
FF_CHUNK = D_FF // 2


def _cparams(*sem):
    return pltpu.CompilerParams(dimension_semantics=sem, vmem_limit_bytes=V7X_VMEM_LIMIT_BYTES)


def _resident(shape):
    nd = len(shape)
    return pl.BlockSpec(shape, lambda *_: (0,) * nd, pipeline_mode=pl.Buffered(1))


def _bdot(a, b):
    return jnp.dot(a.astype(BF16), b.astype(BF16), preferred_element_type=F32)


def _bdot_nt(a, b):
    return lax.dot_general(a.astype(BF16), b.astype(BF16), (((1,), (1,)), ((), ())),
                           preferred_element_type=F32)


def _sigmoid(x):
    return 1.0 / (1.0 + jnp.exp(-x))


def _silu(x):
    return x * _sigmoid(x)


def _softplus(x):
    return jnp.maximum(x, 0.0) + jnp.log(1.0 + jnp.exp(-jnp.abs(x)))


def _gelu_tanh(x):
    return 0.5 * x * (1.0 + jnp.tanh(math.sqrt(2.0 / math.pi) * (x + 0.044715 * (x * x * x))))


def _layer_norm(r, g, b):
    mu = jnp.mean(r, axis=-1, keepdims=True)
    d = r - mu
    var = jnp.mean(d * d, axis=-1, keepdims=True)
    return d * lax.rsqrt(var + LN_EPS) * g + b


def _row_to_col(r):
    n = r.shape[-1]
    eye = lax.broadcasted_iota(jnp.int32, (n, n), 0) == lax.broadcasted_iota(jnp.int32, (n, n), 1)
    return jnp.sum(jnp.where(eye, r, 0.0), axis=-1, keepdims=True)


def _interleave_rows(parts_ref, slab_ref, dil):
    n, w = parts_ref.shape[1], parts_ref.shape[2]
    for r in range(dil):
        for c in range(w // LANES):
            slab_ref[c, pl.ds(r, n, stride=dil), :] = parts_ref[r, :, c * LANES:(c + 1) * LANES].astype(F32)
    return jnp.concatenate([slab_ref[c] for c in range(w // LANES)], axis=-1)


def _proj_tail(mm, wm_ref, wg_ref, dn_ref, z_ref, ba_ref, gate_ref):
    for lo in range(0, DN_QKV_W, DN_W):
        dn_ref[:, lo:lo + DN_W] = mm(wm_ref, OFF_DN_QKV + lo, DN_W).astype(dn_ref.dtype)
    z_ref[...] = mm(wm_ref, OFF_DN_Z, DN_W).astype(z_ref.dtype)
    ba_ref[...] = mm(wm_ref, OFF_DN_BETA, LANES)
    for lo in range(0, 2 * D_MODEL, 512):
        gate_ref[:, lo:lo + 512] = mm(wg_ref, lo, 512).astype(gate_ref.dtype)


def _proj_prompt_kernel(x_ref, wm_ref, wg_ref, a0_ref, a1_ref, a2_ref, dn_ref, z_ref, ba_ref, gate_ref, slab):
    xb = x_ref[...].astype(BF16)
    tm = xb.shape[0]

    def mm(w_ref, lo, width):
        return jnp.dot(xb, w_ref[:, lo:lo + width], preferred_element_type=F32)

    for gi, a_ref in enumerate((a0_ref, a1_ref, a2_ref)):
        dil = DIL_GROUPS[gi][1]
        res = mm(wm_ref, gi * QKV_W, QKV_W)
        if dil == 1:
            a_ref[0] = res.astype(a_ref.dtype)
            continue
        for c in range(QKV_W // LANES):
            slab[c] = res[:, c * LANES:(c + 1) * LANES]
        for r in range(dil):
            for c in range(QKV_W // LANES):
                a_ref[r, :, c * LANES:(c + 1) * LANES] = slab[c, pl.ds(r, tm // dil, stride=dil), :].astype(a_ref.dtype)
    _proj_tail(mm, wm_ref, wg_ref, dn_ref, z_ref, ba_ref, gate_ref)


def _proj_prompt(x2d, wm, wg, bsz, t):
    tm = ROW_TILE
    nt = t // tm
    row = lambda w: pl.BlockSpec((tm, w), lambda i: (i, 0))
    att_specs, att_shapes = [], []
    for _, dil in DIL_GROUPS:
        att_specs.append(pl.BlockSpec((None, dil, tm // dil, QKV_W), lambda i: (i // nt, 0, i % nt, 0)))
        att_shapes.append(jax.ShapeDtypeStruct((bsz, dil, t // dil, QKV_W), BF16))
    m = bsz * t
    return pl.pallas_call(
        _proj_prompt_kernel,
        grid=(m // tm,),
        in_specs=[row(D_MODEL), _resident(wm.shape), _resident(wg.shape)],
        out_specs=att_specs + [row(DN_QKV_W), row(DN_W), row(LANES), row(2 * D_MODEL)],
        out_shape=att_shapes + [jax.ShapeDtypeStruct((m, DN_QKV_W), BF16),
                                jax.ShapeDtypeStruct((m, DN_W), BF16),
                                jax.ShapeDtypeStruct((m, LANES), F32),
                                jax.ShapeDtypeStruct((m, 2 * D_MODEL), BF16)],
        scratch_shapes=[pltpu.VMEM((QKV_W // LANES, tm, LANES), F32)],
        compiler_params=_cparams("arbitrary"),
        name="in_proj_prompt",
    )(x2d, wm, wg)


def _proj_rows_kernel(x_ref, wm_ref, wg_ref, att_ref, dn_ref, z_ref, ba_ref, gate_ref):
    xb = x_ref[...].astype(BF16)

    def mm(w_ref, lo, width):
        return jnp.dot(xb, w_ref[:, lo:lo + width], preferred_element_type=F32)

    for lo in range(0, ATT_QKV_W, QKV_W):
        att_ref[:, lo:lo + QKV_W] = mm(wm_ref, lo, QKV_W)
    _proj_tail(mm, wm_ref, wg_ref, dn_ref, z_ref, ba_ref, gate_ref)


def _proj_rows(x2d, wm, wg):
    m = x2d.shape[0]
    full = lambda w: pl.BlockSpec((m, w), lambda i: (0, 0))
    widths = (ATT_QKV_W, DN_QKV_W, DN_W, LANES, 2 * D_MODEL)
    return pl.pallas_call(
        _proj_rows_kernel,
        grid=(1,),
        in_specs=[full(D_MODEL), _resident(wm.shape), _resident(wg.shape)],
        out_specs=[full(w) for w in widths],
        out_shape=[jax.ShapeDtypeStruct((m, w), F32) for w in widths],
        compiler_params=_cparams("arbitrary"),
        name="in_proj_rows",
    )(x2d, wm, wg)


def _alibi_slopes(group):
    return tuple(2.0 ** (-ALIBI_EXP * (group * HEADS + h + 1) / (N_GROUPS * HEADS)) for h in range(HEADS))


def _attn_kernel(q_ref, kc_ref, kp_ref, vc_ref, vp_ref, o_ref, lse_ref, kbuf, vbuf, *, dil, slopes, nsub):
    sb = ATT_STEPS
    n = pl.program_id(2)
    kbuf[0:sb, :] = kp_ref[...]
    kbuf[sb:, :] = kc_ref[...]
    vbuf[0:sb, :] = vp_ref[...]
    vbuf[sb:, :] = vc_ref[...]

    qi = lax.broadcasted_iota(jnp.int32, (sb, 2 * sb), 0)
    kj = lax.broadcasted_iota(jnp.int32, (sb, 2 * sb), 1)
    dist = sb + qi - kj
    in_band = (dist >= 0) & (dist <= sb)
    first_ok = in_band & (kj >= jnp.where(n == 0, sb, 0))
    distf = (dil * dist).astype(F32)
    bias_rest = [jnp.where(in_band, -slopes[h] * distf, NEG_BIG) for h in range(HEADS)]
    bias_first = [jnp.where(first_ok, -slopes[h] * distf, NEG_BIG) for h in range(HEADS)]
    lane_head = lax.broadcasted_iota(jnp.int32, (sb, GROUP_W), 1) // HEAD_DIM

    for j in range(nsub):
        q = q_ref[j * sb:(j + 1) * sb, :].astype(F32) * (HEAD_DIM ** -0.5)
        kk = kbuf[j * sb:(j + 2) * sb, :]
        vv = vbuf[j * sb:(j + 2) * sb, :]
        qs = jnp.concatenate([jnp.where(lane_head == h, q, 0.0) for h in range(HEADS)], axis=0)
        s = _bdot_nt(qs, kk)
        bias = bias_first if j == 0 else bias_rest
        probs, lses = [], []
        for h in range(HEADS):
            sh = s[h * sb:(h + 1) * sb, :] + bias[h]
            m = jnp.max(sh, axis=-1, keepdims=True)
            p = jnp.exp(sh - m)
            l = jnp.sum(p, axis=-1, keepdims=True)
            probs.append((p * (1.0 / l)).astype(BF16))
            lses.append(m + jnp.log(l))
        pv = jnp.dot(jnp.concatenate(probs, axis=0), vv, preferred_element_type=F32)
        o = jnp.zeros((sb, GROUP_W), F32)
        lse = jnp.zeros((sb, GROUP_W), F32)
        for h in range(HEADS):
            o = jnp.where(lane_head == h, pv[h * sb:(h + 1) * sb, :], o)
            lse = jnp.where(lane_head == h, lses[h], lse)
        o_ref[j * sb:(j + 1) * sb, :] = o.astype(o_ref.dtype)
        lse_ref[j * sb:(j + 1) * sb, :] = lse


def _attn_prompt_group(att_g, group):
    bsz, dil, n_sub_rows, _ = att_g.shape
    sb = ATT_STEPS
    nsub = min(ATT_SUB_BLOCKS, n_sub_rows // sb)
    qb = nsub * sb
    assert n_sub_rows % qb == 0 and dil == DIL_GROUPS[group][1]
    cur = lambda sect: pl.BlockSpec((None, None, qb, GROUP_W), lambda b, r, n: (b, r, n, sect))
    prev = lambda sect: pl.BlockSpec((None, None, sb, GROUP_W),
                                     lambda b, r, n: (b, r, jnp.maximum(n * nsub - 1, 0), sect))
    out_spec = pl.BlockSpec((None, None, qb, GROUP_W), lambda b, r, n: (b, r, n, 0))
    return pl.pallas_call(
        functools.partial(_attn_kernel, dil=dil, slopes=_alibi_slopes(group), nsub=nsub),
        grid=(bsz, dil, n_sub_rows // qb),
        in_specs=[cur(0), cur(1), prev(1), cur(2), prev(2)],
        out_specs=[out_spec, out_spec],
        out_shape=[jax.ShapeDtypeStruct((bsz, dil, n_sub_rows, GROUP_W), BF16),
                   jax.ShapeDtypeStruct((bsz, dil, n_sub_rows, GROUP_W), F32)],
        scratch_shapes=[pltpu.VMEM((qb + sb, GROUP_W), att_g.dtype), pltpu.VMEM((qb + sb, GROUP_W), att_g.dtype)],
        compiler_params=_cparams("arbitrary", "arbitrary", "arbitrary"),
        name=f"attn_prompt_g{group}",
    )(att_g, att_g, att_g, att_g, att_g)


def _kv_tail_kernel(a_ref, out_ref, slab, *, dil):
    if dil == 1:
        rows = a_ref[0].astype(F32)
    else:
        rows = _interleave_rows(a_ref, slab, dil)
    out_ref[...] = rows.T


def _kv_tail(att_g, group):
    bsz, dil, n_sub_rows, _ = att_g.shape
    sb = ATT_STEPS
    keep = sb * dil
    assert n_sub_rows % sb == 0, "prompt shorter than the window is not supported"
    return pl.pallas_call(
        functools.partial(_kv_tail_kernel, dil=dil),
        grid=(bsz, 2),
        in_specs=[pl.BlockSpec((None, dil, sb, GROUP_W), lambda b, kv: (b, 0, n_sub_rows // sb - 1, 1 + kv))],
        out_specs=pl.BlockSpec((None, None, GROUP_W, keep), lambda b, kv: (b, kv, 0, 0)),
        out_shape=jax.ShapeDtypeStruct((bsz, 2, GROUP_W, keep), F32),
        scratch_shapes=[pltpu.VMEM((GROUP_W // LANES, keep, LANES), F32)],
        compiler_params=_cparams("arbitrary", "arbitrary"),
        name=f"kv_tail_g{group}",
    )(att_g)


def _dn_gates(ba, alog, dtb):
    beta = _sigmoid(ba)
    g = -jnp.exp(alog) * _softplus(ba + dtb)
    return beta, g


def _l2norm(x):
    return x * lax.rsqrt(jnp.sum(x * x, axis=-1, keepdims=True) + RMS_EPS)


def _gated_rmsnorm(o, z, w):
    return o * lax.rsqrt(jnp.mean(o * o, axis=-1, keepdims=True) + RMS_EPS) * w * _silu(z)


def _dn_prompt_kernel(dn_ref, ba_ref, z_ref, wconv_ref, alog_ref, dtb_ref, onw_ref,
                      o_ref, cst_ref, sfin_ref, xbuf, s_scr, *, nb):
    c = pl.program_id(0)
    ch = DN_CHUNK
    halo = SUBLANES

    @pl.when(c == 0)
    def _():
        xbuf[...] = jnp.zeros_like(xbuf)
        s_scr[...] = jnp.zeros_like(s_scr)

    row = lax.broadcasted_iota(jnp.int32, (ch, ch), 0)
    col = lax.broadcasted_iota(jnp.int32, (ch, ch), 1)
    lower = row >= col
    strict = row > col
    row_l = lax.broadcasted_iota(jnp.int32, (ch, LANES), 0)
    w = wconv_ref[...]
    onw = onw_ref[...]

    acts, betas, gcs, gcts, egcs, eglasts, ekds = [], [], [], [], [], [], []
    for b in range(nb):
        x = dn_ref[b].astype(F32)
        xbuf[b, halo:halo + ch, :] = x
        conv = w[DN_CONV - 1:DN_CONV, :] * x
        for j in range(DN_CONV - 1):
            conv = conv + w[j:j + 1, :] * xbuf[b, halo - (DN_CONV - 1) + j:halo - (DN_CONV - 1) + j + ch, :]
        tail = xbuf[b, halo + ch - (DN_CONV - 1):halo + ch, :]
        cst_ref[b] = tail
        xbuf[b, halo - (DN_CONV - 1):halo, :] = tail
        acts.append(_silu(conv))
        beta_all, g_all = _dn_gates(ba_ref[b], alog_ref[...], dtb_ref[...])
        gc = g_all
        for sft in (1, 2, 4, 8, 16, 32):
            gc = gc + jnp.where(row_l >= sft, pltpu.roll(gc, sft, 0), 0.0)
        glast = gc[ch - 1:ch, :]
        betas.append(beta_all)
        gcs.append(gc)
        gcts.append(gc.T)
        egcs.append(jnp.exp(gc))
        eglasts.append(jnp.exp(glast))
        ekds.append(jnp.exp(glast - gc))

    pairs = [(b, h) for b in range(nb) for h in range(DN_HEADS)]
    ga = lambda h: DN_HEADS + h
    head = lambda a, sect, h: a[:, sect * DN_W + h * DN_DIM:sect * DN_W + (h + 1) * DN_DIM]
    q = [_l2norm(head(acts[b], 0, h)) * (DN_DIM ** -0.5) for b, h in pairs]
    k = [_l2norm(head(acts[b], 1, h)) for b, h in pairs]
    v = [head(acts[b], 2, h) for b, h in pairs]
    beta = [betas[b][:, h:h + 1] for b, h in pairs]
    eg = [egcs[b][:, ga(h):ga(h) + 1] for b, h in pairs]
    decay = [jnp.where(lower, jnp.exp(jnp.minimum(gcs[b][:, ga(h):ga(h) + 1] - gcts[b][ga(h):ga(h) + 1, :], 0.0)), 0.0)
             for b, h in pairs]
    kb = [ki * bi for ki, bi in zip(k, beta)]
    kq = [_bdot_nt(jnp.concatenate([kbi, qi], axis=0), ki) for kbi, qi, ki in zip(kb, q, k)]
    xm = [jnp.where(strict, -(kqi[:ch] * di), 0.0) for kqi, di in zip(kq, decay)]
    attn = [kqi[ch:] * di for kqi, di in zip(kq, decay)]
    sol = [jnp.concatenate([vi * bi, kbi * egi], axis=-1) for vi, bi, kbi, egi in zip(v, beta, kb, eg)]
    n_levels = int(math.log2(ch))
    for lvl in range(n_levels):
        if lvl < n_levels - 1:
            prod = [_bdot(xi, jnp.concatenate([si, xi], axis=-1)) for xi, si in zip(xm, sol)]
            sol = [si + pi[:, :2 * DN_DIM] for si, pi in zip(sol, prod)]
            xm = [pi[:, 2 * DN_DIM:] for pi in prod]
        else:
            sol = [si + _bdot(xi, si) for xi, si in zip(xm, sol)]
    s_old = [s_scr[b, h] for b, h in pairs]
    ws = [_bdot(jnp.concatenate([si[:, DN_DIM:], qi * egi], axis=0), so)
          for si, qi, egi, so in zip(sol, q, eg, s_old)]
    u = [si[:, :DN_DIM] - wi[:ch] for si, wi in zip(sol, ws)]
    kdt = [(ki * ekds[b][:, ga(h):ga(h) + 1]).T for ki, (b, h) in zip(k, pairs)]
    au = [_bdot(jnp.concatenate([ai, kdi], axis=0), ui) for ai, kdi, ui in zip(attn, kdt, u)]
    for i, (b, h) in enumerate(pairs):
        s_scr[b, h] = s_old[i] * eglasts[b][:, ga(h):ga(h) + 1] + au[i][ch:]
        o = ws[i][ch:] + au[i][:ch]
        zf = z_ref[b, :, h * DN_DIM:(h + 1) * DN_DIM].astype(F32)
        o_ref[b, :, h * DN_DIM:(h + 1) * DN_DIM] = _gated_rmsnorm(o, zf, onw).astype(o_ref.dtype)

    @pl.when(c == pl.num_programs(0) - 1)
    def _():
        sfin_ref[...] = s_scr[...]


def _dn_prompt(dn, ba, z, wconv, alog128, dtb128, onw, bsz, t):
    ch = DN_CHUNK
    blk = lambda w: pl.BlockSpec((bsz, ch, w), lambda c: (0, c, 0))
    return pl.pallas_call(
        functools.partial(_dn_prompt_kernel, nb=bsz),
        grid=(t // ch,),
        in_specs=[blk(DN_QKV_W), blk(LANES), blk(DN_W),
                  _resident(wconv.shape), _resident(alog128.shape), _resident(dtb128.shape), _resident(onw.shape)],
        out_specs=[blk(DN_W),
                   pl.BlockSpec((bsz, DN_CONV - 1, DN_QKV_W), lambda c: (0, 0, 0)),
                   pl.BlockSpec((bsz, DN_HEADS, DN_DIM, DN_DIM), lambda c: (0, 0, 0, 0))],
        out_shape=[jax.ShapeDtypeStruct((bsz, t, DN_W), BF16),
                   jax.ShapeDtypeStruct((bsz, DN_CONV - 1, DN_QKV_W), F32),
                   jax.ShapeDtypeStruct((bsz, DN_HEADS, DN_DIM, DN_DIM), F32)],
        scratch_shapes=[pltpu.VMEM((bsz, SUBLANES + ch, DN_QKV_W), F32),
                        pltpu.VMEM((bsz, DN_HEADS, DN_DIM, DN_DIM), F32)],
        compiler_params=_cparams("arbitrary"),
        name="deltanet_prompt",
    )(dn.reshape(bsz, t, DN_QKV_W), ba.reshape(bsz, t, LANES), z.reshape(bsz, t, DN_W),
      wconv, alog128, dtb128, onw)


def _merge_tail(o_att, odn_ref, gate_ref, x_ref, wa_ref, wd_ref, wo_ref, g_ref, b_ref, h_ref, alpha):
    a = jnp.dot(o_att.astype(BF16), wa_ref[...], preferred_element_type=F32)
    bm = jnp.dot(odn_ref[...].astype(BF16), wd_ref[...], preferred_element_type=F32)
    sg = _sigmoid(gate_ref[...].astype(F32))
    merged = sg[:, :D_MODEL] * a + sg[:, D_MODEL:] * bm
    mix = jnp.dot(merged.astype(BF16), wo_ref[...], preferred_element_type=F32)
    h_ref[...] = _layer_norm(alpha * x_ref[...] + mix, g_ref[...], b_ref[...])


def _merge_prompt_kernel(o0_ref, l0_ref, o1_ref, l1_ref, o2_ref, l2_ref, odn_ref, gate_ref, x_ref,
                         wa_ref, wd_ref, wo_ref, g_ref, b_ref, h_ref, slabs, *, alpha):
    outs, lses = [], []
    for gi, (o_ref, l_ref) in enumerate(((o0_ref, l0_ref), (o1_ref, l1_ref), (o2_ref, l2_ref))):
        dil = DIL_GROUPS[gi][1]
        if dil == 1:
            outs.append(o_ref[0].astype(F32))
            lses.append(l_ref[0])
        else:
            outs.append(_interleave_rows(o_ref, slabs.at[2 * (gi - 1)], dil))
            lses.append(_interleave_rows(l_ref, slabs.at[2 * (gi - 1) + 1], dil))
    m = functools.reduce(jnp.maximum, lses)
    es = [jnp.exp(l - m) for l in lses]
    o_att = sum(e * o for e, o in zip(es, outs)) * (1.0 / sum(es))
    _merge_tail(o_att, odn_ref, gate_ref, x_ref, wa_ref, wd_ref, wo_ref, g_ref, b_ref, h_ref, alpha)


def _merge_prompt(att_parts, odn, gates, x2d, wa, wd, wo, g, b, bsz, t, alpha):
    tm = ROW_TILE
    nt = t // tm
    row = lambda w: pl.BlockSpec((tm, w), lambda i: (i, 0))
    att_specs = []
    for _, dil in DIL_GROUPS:
        att_specs += [pl.BlockSpec((None, dil, tm // dil, GROUP_W), lambda i: (i // nt, 0, i % nt, 0))] * 2
    return pl.pallas_call(
        functools.partial(_merge_prompt_kernel, alpha=alpha),
        grid=(bsz * nt,),
        in_specs=att_specs + [row(DN_W), row(2 * D_MODEL), row(D_MODEL),
                              _resident(wa.shape), _resident(wd.shape), _resident(wo.shape),
                              _resident(g.shape), _resident(b.shape)],
        out_specs=row(D_MODEL),
        out_shape=jax.ShapeDtypeStruct((bsz * t, D_MODEL), F32),
        scratch_shapes=[pltpu.VMEM((2 * (N_GROUPS - 1), GROUP_W // LANES, tm, LANES), F32)],
        compiler_params=_cparams("arbitrary"),
        name="merge_ln1_prompt",
    )(*att_parts, odn, gates, x2d, wa, wd, wo, g, b)


def _merge_rows_kernel(oatt_ref, odn_ref, gate_ref, x_ref, wa_ref, wd_ref, wo_ref, g_ref, b_ref, h_ref, *, alpha):
    _merge_tail(oatt_ref[...], odn_ref, gate_ref, x_ref, wa_ref, wd_ref, wo_ref, g_ref, b_ref, h_ref, alpha)


def _merge_rows(oatt, odn, gates, x2d, wa, wd, wo, g, b, alpha):
    m = x2d.shape[0]
    ins = (oatt, odn, gates, x2d, wa, wd, wo, g, b)
    return pl.pallas_call(
        functools.partial(_merge_rows_kernel, alpha=alpha),
        grid=(1,),
        in_specs=[pl.BlockSpec(a.shape, lambda i: (0, 0)) for a in ins],
        out_specs=pl.BlockSpec((m, D_MODEL), lambda i: (0, 0)),
        out_shape=jax.ShapeDtypeStruct((m, D_MODEL), F32),
        compiler_params=_cparams("arbitrary"),
        name="merge_ln1_rows",
    )(*ins)


def _ffn_prompt_kernel(h_ref, wup_ref, wconv_ref, bconv_ref, wdn_ref, g_ref, b_ref,
                       y_ref, cst_ref, halo, *, alpha):
    @pl.when(pl.program_id(1) == 0)
    def _():
        halo[...] = jnp.zeros_like(halo)

    h = h_ref[...]
    tm = h.shape[0]
    hb = h.astype(BF16)
    row = lax.broadcasted_iota(jnp.int32, (tm, FF_CHUNK), 0)
    acc = jnp.zeros((tm, D_MODEL), F32)
    for c in range(D_FF // FF_CHUNK):
        cols = slice(c * FF_CHUNK, (c + 1) * FF_CHUNK)
        a = jnp.dot(hb, wup_ref[:, cols], preferred_element_type=F32)
        gt = jnp.dot(hb, wup_ref[:, D_FF + c * FF_CHUNK:D_FF + (c + 1) * FF_CHUNK], preferred_element_type=F32)
        prev2 = halo[0:1, cols]
        prev1 = halo[1:2, cols]
        a1 = jnp.where(row == 0, prev1, pltpu.roll(a, 1, 0))
        a2 = jnp.where(row == 0, prev2, jnp.where(row == 1, prev1, pltpu.roll(a, 2, 0)))
        wc = wconv_ref[:, cols]
        conv = wc[0:1, :] * a2 + wc[1:2, :] * a1 + wc[2:3, :] * a + bconv_ref[:, cols]
        act = _gelu_tanh(conv) * gt
        acc = acc + jnp.dot(act.astype(BF16), wdn_ref[cols, :], preferred_element_type=F32)
        tail = a[tm - (FFN_CONV - 1):tm, :]
        halo[0:FFN_CONV - 1, cols] = tail
        cst_ref[:, cols] = tail
    y_ref[...] = _layer_norm(alpha * h + acc, g_ref[...], b_ref[...])


def _ffn_prompt(h2d, wup, wconv, bconv, wdn, g, b, bsz, t, alpha):
    tm = ROW_TILE
    nt = t // tm
    return pl.pallas_call(
        functools.partial(_ffn_prompt_kernel, alpha=alpha),
        grid=(bsz, nt),
        in_specs=[pl.BlockSpec((tm, D_MODEL), lambda bi, i: (bi * nt + i, 0)),
                  _resident(wup.shape), _resident(wconv.shape), _resident(bconv.shape), _resident(wdn.shape),
                  _resident(g.shape), _resident(b.shape)],
        out_specs=[pl.BlockSpec((tm, D_MODEL), lambda bi, i: (bi * nt + i, 0)),
                   pl.BlockSpec((None, FFN_CONV - 1, D_FF), lambda bi, i: (bi, 0, 0))],
        out_shape=[jax.ShapeDtypeStruct((bsz * t, D_MODEL), F32),
                   jax.ShapeDtypeStruct((bsz, FFN_CONV - 1, D_FF), F32)],
        scratch_shapes=[pltpu.VMEM((SUBLANES, D_FF), F32)],
        compiler_params=_cparams("arbitrary", "arbitrary"),
        name="convffn_prompt",
    )(h2d, wup, wconv, bconv, wdn, g, b)


def _ffn_step_kernel(h_ref, prev_ref, wup_ref, wconv_ref, bconv_ref, wdn_ref, g_ref, b_ref,
                     y_ref, st_ref, *, alpha):
    h = h_ref[...]
    hb = h.astype(BF16)
    a = jnp.dot(hb, wup_ref[:, :D_FF], preferred_element_type=F32)
    gt = jnp.dot(hb, wup_ref[:, D_FF:], preferred_element_type=F32)
    p0 = prev_ref[:, :D_FF]
    p1 = prev_ref[:, D_FF:]
    wc = wconv_ref[...]
    conv = wc[0:1, :] * p0 + wc[1:2, :] * p1 + wc[2:3, :] * a + bconv_ref[...]
    act = _gelu_tanh(conv) * gt
    f = jnp.dot(act.astype(BF16), wdn_ref[...], preferred_element_type=F32)
    y_ref[...] = _layer_norm(alpha * h + f, g_ref[...], b_ref[...])
    st_ref[:, :D_FF] = p1
    st_ref[:, D_FF:] = a


def _ffn_step(h2d, prev2d, wup, wconv, bconv, wdn, g, b, alpha):
    m = h2d.shape[0]
    full = lambda a: pl.BlockSpec(a.shape, lambda i: (0,) * a.ndim)
    ins = (h2d, prev2d, wup, wconv, bconv, wdn, g, b)
    return pl.pallas_call(
        functools.partial(_ffn_step_kernel, alpha=alpha),
        grid=(1,),
        in_specs=[full(a) for a in ins],
        out_specs=[pl.BlockSpec((m, D_MODEL), lambda i: (0, 0)),
                   pl.BlockSpec((m, (FFN_CONV - 1) * D_FF), lambda i: (0, 0))],
        out_shape=[jax.ShapeDtypeStruct((m, D_MODEL), F32),
                   jax.ShapeDtypeStruct((m, (FFN_CONV - 1) * D_FF), F32)],
        compiler_params=_cparams("arbitrary"),
        name="convffn_step",
    )(*ins)


def _mixer_step_kernel(att_ref, dn_ref, z_ref, ba_ref, c0_ref, c1_ref, c2_ref, cst_ref, s_ref,
                       wconv_ref, alog_ref, dtb_ref, onw_ref,
                       oatt_ref, odn_ref, n0_ref, n1_ref, n2_ref, ncst_ref, ns_ref):
    b = pl.program_id(0)
    me = pl.ds(b, 1)
    att = att_ref[me, :]
    sub = lax.broadcasted_iota(jnp.int32, (SUBLANES, GROUP_W), 0)
    lane_head = lax.broadcasted_iota(jnp.int32, (SUBLANES, GROUP_W), 1) // HEAD_DIM
    head_sel = sub == lane_head

    o_parts, lse_parts = [], []
    for gi, (cache_ref, new_ref) in enumerate(((c0_ref, n0_ref), (c1_ref, n1_ref), (c2_ref, n2_ref))):
        window, dil = DIL_GROUPS[gi]
        n = cache_ref.shape[-1]
        assert n == window == ATT_STEPS * dil, "cache shorter than the window is not supported"
        q = att[:, gi * QKV_W:gi * QKV_W + GROUP_W] * (HEAD_DIM ** -0.5)
        k_new = att[:, gi * QKV_W + GROUP_W:gi * QKV_W + 2 * GROUP_W]
        v_new = att[:, gi * QKV_W + 2 * GROUP_W:(gi + 1) * QKV_W]
        kt = cache_ref[0]
        vt = cache_ref[1]
        qm = jnp.where(head_sel, jnp.broadcast_to(q, (SUBLANES, GROUP_W)), 0.0)
        pos = lax.broadcasted_iota(jnp.int32, (SUBLANES, n), 1)
        sub_n = lax.broadcasted_iota(jnp.int32, (SUBLANES, n), 0)
        slopes = _alibi_slopes(gi)
        slope = jnp.zeros((SUBLANES, n), F32)
        for h in range(HEADS):
            slope = jnp.where(sub_n == h, slopes[h], slope)
        on_grid = (pos & (dil - 1)) == 0
        bias = jnp.where(on_grid, -slope * (n - pos).astype(F32), NEG_BIG)
        s_old = _bdot(qm, kt) + bias
        s_new = jnp.sum(qm * k_new, axis=-1, keepdims=True)
        m = jnp.maximum(jnp.max(s_old, axis=-1, keepdims=True), s_new)
        p_old = jnp.exp(s_old - m)
        p_new = jnp.exp(s_new - m)
        l = jnp.sum(p_old, axis=-1, keepdims=True) + p_new
        inv_l = 1.0 / l
        o8 = _bdot_nt(p_old * inv_l, vt) + (p_new * inv_l) * v_new
        o_parts.append(o8)
        lse_parts.append(m + jnp.log(l))
        lane = lax.broadcasted_iota(jnp.int32, (GROUP_W, n), 1)
        new_ref[0] = jnp.where(lane == n - 1, _row_to_col(k_new), pltpu.roll(kt, n - 1, 1))
        new_ref[1] = jnp.where(lane == n - 1, _row_to_col(v_new), pltpu.roll(vt, n - 1, 1))
    mx = functools.reduce(jnp.maximum, lse_parts)
    es = [jnp.exp(l - mx) for l in lse_parts]
    comb = sum(e * o for e, o in zip(es, o_parts)) * (1.0 / sum(es))
    oatt_ref[me, :] = jnp.sum(jnp.where(head_sel, comb, 0.0), axis=0, keepdims=True)

    x_new = dn_ref[me, :]
    w = wconv_ref[...]
    conv = w[DN_CONV - 1:DN_CONV, :] * x_new
    for j in range(DN_CONV - 1):
        old = cst_ref[j, me, :]
        conv = conv + w[j:j + 1, :] * old
        if j > 0:
            ncst_ref[j - 1, me, :] = old
    ncst_ref[DN_CONV - 2, me, :] = x_new
    act = _silu(conv)
    beta_all, g_all = _dn_gates(ba_ref[me, :], alog_ref[...], dtb_ref[...])
    z = z_ref[me, :]
    outs = []
    for h in range(DN_HEADS):
        q = _l2norm(act[:, h * DN_DIM:(h + 1) * DN_DIM]) * (DN_DIM ** -0.5)
        k = _l2norm(act[:, DN_W + h * DN_DIM:DN_W + (h + 1) * DN_DIM])
        v = act[:, 2 * DN_W + h * DN_DIM:2 * DN_W + (h + 1) * DN_DIM]
        beta = beta_all[:, h:h + 1]
        s = s_ref[h] * jnp.exp(g_all[:, DN_HEADS + h:DN_HEADS + h + 1])
        k_col = _row_to_col(k)
        u = beta * (v - jnp.sum(k_col * s, axis=0, keepdims=True))
        s = s + k_col * u
        ns_ref[h] = s
        o = jnp.sum(_row_to_col(q) * s, axis=0, keepdims=True)
        outs.append(_gated_rmsnorm(o, z[:, h * DN_DIM:(h + 1) * DN_DIM], onw_ref[...]))
    odn_ref[me, :] = jnp.concatenate(outs, axis=-1)


def _mixer_step(att, dn, z, ba, caches_t, cst_t, s0, wconv, alog128, dtb128, onw):
    nb = att.shape[0]
    per_b = lambda a: pl.BlockSpec((None,) + a.shape[1:], lambda b: (b,) + (0,) * (a.ndim - 1))
    whole = lambda a: pl.BlockSpec(a.shape, lambda b: (0,) * a.ndim)
    ins = (att, dn, z, ba, *caches_t, cst_t, s0, wconv, alog128, dtb128, onw)
    in_specs = ([whole(a) for a in (att, dn, z, ba)] + [per_b(c) for c in caches_t]
                + [whole(cst_t), per_b(s0)] + [_resident(a.shape) for a in (wconv, alog128, dtb128, onw)])
    outs = [jax.ShapeDtypeStruct((nb, GROUP_W), F32), jax.ShapeDtypeStruct((nb, DN_W), F32),
            *[jax.ShapeDtypeStruct(c.shape, F32) for c in caches_t],
            jax.ShapeDtypeStruct(cst_t.shape, F32), jax.ShapeDtypeStruct(s0.shape, F32)]
    out_specs = ([whole(outs[0]), whole(outs[1])] + [per_b(c) for c in caches_t] + [whole(cst_t), per_b(s0)])
    return pl.pallas_call(
        _mixer_step_kernel,
        grid=(nb,),
        in_specs=in_specs,
        out_specs=out_specs,
        out_shape=outs,
        compiler_params=_cparams("arbitrary"),
        name="mixer_step",
    )(*ins)


def _pad_lanes(v, offset):
    return jnp.zeros((1, LANES), F32).at[0, offset:offset + v.shape[0]].set(v.astype(F32))


def _main_weight(w_in_l):
    att = w_in_l[:, :ATT_QKV_W].reshape(D_MODEL, 3, N_GROUPS, GROUP_W).transpose(0, 2, 1, 3).reshape(D_MODEL, ATT_QKV_W)
    return jnp.concatenate([att, w_in_l[:, ATT_QKV_W:W_MAIN]], axis=1).astype(BF16)


def kernel(x_prompt, x_sample, cache_kv_w128, cache_kv_w512, cache_kv_w2048, state_dn_conv, state_dn_S,
           state_ffn_conv, w_in, w_dn_conv, dn_a_log, dn_dt_bias, dn_onorm_w, w_att_out, w_dn_out, w_o,
           ln1_g, ln1_b, w_up, w_ffn_conv, b_ffn_conv, w_down, ln2_g, ln2_b):
    depth = w_in.shape[0]
    alpha = (2 * depth) ** 0.25
    bsz, t, _ = x_prompt.shape
    nb = x_sample.shape[0]
    assert x_sample.shape[1] == 1 and t % ROW_TILE == 0
    caches_in = (cache_kv_w128, cache_kv_w512, cache_kv_w2048)

    yp = x_prompt.reshape(bsz * t, D_MODEL)
    ys = x_sample.reshape(nb, D_MODEL)
    p_states, s_states = [], []
    for l in range(depth):
        wm = _main_weight(w_in[l])
        wg = w_in[l, :, OFF_GATE:].astype(BF16)
        wa, wd, wo = w_att_out[l].astype(BF16), w_dn_out[l].astype(BF16), w_o[l].astype(BF16)
        wup, wdn = w_up[l].astype(BF16), w_down[l].astype(BF16)
        wconv = w_dn_conv[l]
        alog128 = _pad_lanes(dn_a_log[l], DN_HEADS)
        dtb128 = _pad_lanes(dn_dt_bias[l], DN_HEADS)
        onw = dn_onorm_w[l].reshape(1, DN_DIM)
        g1, b1 = ln1_g[l].reshape(1, D_MODEL), ln1_b[l].reshape(1, D_MODEL)
        g2, b2 = ln2_g[l].reshape(1, D_MODEL), ln2_b[l].reshape(1, D_MODEL)
        wfc, bfc = w_ffn_conv[l], b_ffn_conv[l].reshape(1, D_FF)

        a0, a1, a2, dn, z, ba, gates = _proj_prompt(yp, wm, wg, bsz, t)
        att_parts, p_kv = [], []
        for gi, a_g in enumerate((a0, a1, a2)):
            att_parts.extend(_attn_prompt_group(a_g, gi))
            tail = _kv_tail(a_g, gi)
            p_kv.append(tail.reshape(bsz, 2, HEADS, HEAD_DIM, -1).transpose(0, 4, 1, 2, 3))
        odn, p_cst, p_s = _dn_prompt(dn, ba, z, wconv, alog128, dtb128, onw, bsz, t)
        hp = _merge_prompt(att_parts, odn.reshape(bsz * t, DN_W), gates, yp, wa, wd, wo, g1, b1, bsz, t, alpha)
        yp_new, p_ffn = _ffn_prompt(hp, wup, wfc, bfc, wdn, g2, b2, bsz, t, alpha)
        p_states.append((*p_kv, p_cst, p_s, p_ffn))

        att_s, dn_s, z_s, ba_s, gates_s = _proj_rows(ys, wm, wg)
        caches_t = [c[l].transpose(0, 2, 3, 4, 1).reshape(nb, 2, GROUP_W, -1) for c in caches_in]
        (oatt_s, odn_s, n0, n1, n2, n_cst, n_s) = _mixer_step(
            att_s, dn_s, z_s, ba_s, caches_t, state_dn_conv[l].transpose(1, 0, 2), state_dn_S[l],
            wconv, alog128, dtb128, onw)
        hs = _merge_rows(oatt_s, odn_s, gates_s, ys, wa, wd, wo, g1, b1, alpha)
        ys_new, s_ffn = _ffn_step(hs, state_ffn_conv[l].reshape(nb, (FFN_CONV - 1) * D_FF),
                                  wup, wfc, bfc, wdn, g2, b2, alpha)
        s_kv = [n.reshape(nb, 2, HEADS, HEAD_DIM, -1).transpose(0, 4, 1, 2, 3) for n in (n0, n1, n2)]
        s_states.append((*s_kv, n_cst.transpose(1, 0, 2), n_s, s_ffn.reshape(nb, FFN_CONV - 1, D_FF)))
        yp, ys = yp_new, ys_new

    stacked = lambda states, i: jnp.stack([st[i] for st in states], axis=0)
    return (yp.reshape(bsz, t, D_MODEL), ys.reshape(nb, 1, D_MODEL),
            *[stacked(p_states, i) for i in range(6)],
            *[stacked(s_states, i) for i in range(6)])
```

```python
import functools
import math

import jax
import jax.numpy as jnp
from jax import lax
from jax.experimental import pallas as pl
from jax.experimental.pallas import tpu as pltpu

F32 = jnp.float32
BF16 = jnp.bfloat16

D_MODEL = 1024
DIL_GROUPS = ((128, 1), (512, 4), (2048, 16))
N_GROUPS = len(DIL_GROUPS)
HEADS = 4
HEAD_DIM = 64
GROUP_W = HEADS * HEAD_DIM
QKV_W = 3 * GROUP_W
ATT_W = N_GROUPS * GROUP_W
ATT_QKV_W = 3 * ATT_W
ALIBI_EXP = 8.0
ATT_STEPS = 128
DN_HEADS = 4
DN_DIM = 128
DN_W = DN_HEADS * DN_DIM
DN_QKV_W = 3 * DN_W
DN_CONV = 4
DN_CHUNK = 64
D_FF = 2816
FFN_CONV = 3
LN_EPS = 1e-5
RMS_EPS = 1e-6
OFF_DN_QKV = ATT_QKV_W
OFF_DN_Z = OFF_DN_QKV + DN_QKV_W
OFF_DN_BETA = OFF_DN_Z + DN_W
OFF_GATE = OFF_DN_BETA + 2 * DN_HEADS
W_MAIN = OFF_DN_BETA + 128
NEG_BIG = -1e30

V7X_VMEM_LIMIT_BYTES = 56 * 1024 * 1024
LANES = 128
SUBLANES = 8

ROW_TILE = 512
ATT_SUB_BLOCKS = 4
FFN_ROW_TILE = 512
FF_CHUNKS = (512, 512, 512, 512, 512, 256)


def _cparams(*sem):
    return pltpu.CompilerParams(dimension_semantics=sem, vmem_limit_bytes=V7X_VMEM_LIMIT_BYTES)


def _resident(shape):
    nd = len(shape)
    return pl.BlockSpec(shape, lambda *_: (0,) * nd, pipeline_mode=pl.Buffered(1))


def _bdot(a, b):
    return jnp.dot(a.astype(BF16), b.astype(BF16), preferred_element_type=F32)


def _bdot_nt(a, b):
    return lax.dot_general(a.astype(BF16), b.astype(BF16), (((1,), (1,)), ((), ())),
                           preferred_element_type=F32)


def _sigmoid(x):
    return 0.5 * jnp.tanh(0.5 * x) + 0.5


def _silu(x):
    return x * _sigmoid(x)


def _softplus(x):
    return jnp.maximum(x, 0.0) + jnp.log(1.0 + jnp.exp(-jnp.abs(x)))


def _gelu_tanh(x):
    c = math.sqrt(2.0 / math.pi)
    hx = 0.5 * x
    return hx + hx * jnp.tanh(x * (c + (c * 0.044715) * (x * x)))


def _layer_norm(r, g, b):
    mu = jnp.mean(r, axis=-1, keepdims=True)
    d = r - mu
    var = jnp.mean(d * d, axis=-1, keepdims=True)
    return d * lax.rsqrt(var + LN_EPS) * g + b


def _row_to_col(r):
    n = r.shape[-1]
    eye = lax.broadcasted_iota(jnp.int32, (n, n), 0) == lax.broadcasted_iota(jnp.int32, (n, n), 1)
    return jnp.sum(jnp.where(eye, r, 0.0), axis=-1, keepdims=True)


def _interleave_rows(parts_ref, slab_ref, dil):
    n, w = parts_ref.shape[1], parts_ref.shape[2]
    for r in range(dil):
        for c in range(w // LANES):
            slab_ref[c, pl.ds(r, n, stride=dil), :] = parts_ref[r, :, c * LANES:(c + 1) * LANES].astype(F32)
    return jnp.concatenate([slab_ref[c] for c in range(w // LANES)], axis=-1)


def _proj_tail(mm, wm_ref, wg_ref, dn_ref, z_ref, ba_ref, gate_ref):
    for lo in range(0, DN_QKV_W, DN_W):
        dn_ref[:, lo:lo + DN_W] = mm(wm_ref, OFF_DN_QKV + lo, DN_W).astype(dn_ref.dtype)
    z_ref[...] = mm(wm_ref, OFF_DN_Z, DN_W).astype(z_ref.dtype)
    ba_ref[...] = mm(wm_ref, OFF_DN_BETA, LANES)
    for lo in range(0, 2 * D_MODEL, 512):
        gate_ref[:, lo:lo + 512] = mm(wg_ref, lo, 512).astype(gate_ref.dtype)


def _proj_prompt_kernel(x_ref, wm_ref, wg_ref, a0_ref, a1_ref, a2_ref, dn_ref, z_ref, ba_ref, gate_ref, slab):
    xb = x_ref[...].astype(BF16)
    tm = xb.shape[0]

    def mm(w_ref, lo, width):
        return jnp.dot(xb, w_ref[:, lo:lo + width], preferred_element_type=F32)

    for gi, a_ref in enumerate((a0_ref, a1_ref, a2_ref)):
        dil = DIL_GROUPS[gi][1]
        res = mm(wm_ref, gi * QKV_W, QKV_W)
        if dil == 1:
            a_ref[0] = res.astype(a_ref.dtype)
            continue
        for c in range(QKV_W // LANES):
            slab[c] = res[:, c * LANES:(c + 1) * LANES]
        for r in range(dil):
            for c in range(QKV_W // LANES):
                a_ref[r, :, c * LANES:(c + 1) * LANES] = slab[c, pl.ds(r, tm // dil, stride=dil), :].astype(a_ref.dtype)
    _proj_tail(mm, wm_ref, wg_ref, dn_ref, z_ref, ba_ref, gate_ref)


def _proj_prompt(x2d, wm, wg, bsz, t):
    tm = ROW_TILE
    nt = t // tm
    row = lambda w: pl.BlockSpec((tm, w), lambda i: (i, 0))
    att_specs, att_shapes = [], []
    for _, dil in DIL_GROUPS:
        att_specs.append(pl.BlockSpec((None, dil, tm // dil, QKV_W), lambda i: (i // nt, 0, i % nt, 0)))
        att_shapes.append(jax.ShapeDtypeStruct((bsz, dil, t // dil, QKV_W), BF16))
    m = bsz * t
    return pl.pallas_call(
        _proj_prompt_kernel,
        grid=(m // tm,),
        in_specs=[row(D_MODEL), _resident(wm.shape), _resident(wg.shape)],
        out_specs=att_specs + [row(DN_QKV_W), row(DN_W), row(LANES), row(2 * D_MODEL)],
        out_shape=att_shapes + [jax.ShapeDtypeStruct((m, DN_QKV_W), BF16),
                                jax.ShapeDtypeStruct((m, DN_W), BF16),
                                jax.ShapeDtypeStruct((m, LANES), F32),
                                jax.ShapeDtypeStruct((m, 2 * D_MODEL), BF16)],
        scratch_shapes=[pltpu.VMEM((QKV_W // LANES, tm, LANES), F32)],
        compiler_params=_cparams("arbitrary"),
        name="in_proj_prompt",
    )(x2d, wm, wg)


def _proj_rows_kernel(x_ref, wm_ref, wg_ref, att_ref, dn_ref, z_ref, ba_ref, gate_ref):
    xb = x_ref[...].astype(BF16)

    def mm(w_ref, lo, width):
        return jnp.dot(xb, w_ref[:, lo:lo + width], preferred_element_type=F32)

    for lo in range(0, ATT_QKV_W, QKV_W):
        att_ref[:, lo:lo + QKV_W] = mm(wm_ref, lo, QKV_W)
    _proj_tail(mm, wm_ref, wg_ref, dn_ref, z_ref, ba_ref, gate_ref)


def _proj_rows(x2d, wm, wg):
    m = x2d.shape[0]
    full = lambda w: pl.BlockSpec((m, w), lambda i: (0, 0))
    widths = (ATT_QKV_W, DN_QKV_W, DN_W, LANES, 2 * D_MODEL)
    return pl.pallas_call(
        _proj_rows_kernel,
        grid=(1,),
        in_specs=[full(D_MODEL), _resident(wm.shape), _resident(wg.shape)],
        out_specs=[full(w) for w in widths],
        out_shape=[jax.ShapeDtypeStruct((m, w), F32) for w in widths],
        compiler_params=_cparams("arbitrary"),
        name="in_proj_rows",
    )(x2d, wm, wg)


def _alibi_slopes(group):
    return tuple(2.0 ** (-ALIBI_EXP * (group * HEADS + h + 1) / (N_GROUPS * HEADS)) for h in range(HEADS))


def _attn_kernel(q_ref, kc_ref, kp_ref, vc_ref, vp_ref, o_ref, lse_ref, kbuf, vbuf, *, dil, slopes, nsub):
    sb = ATT_STEPS
    n = pl.program_id(2)
    kbuf[0:sb, :] = kp_ref[...]
    kbuf[sb:, :] = kc_ref[...]
    vbuf[0:sb, :] = vp_ref[...]
    vbuf[sb:, :] = vc_ref[...]

    qi = lax.broadcasted_iota(jnp.int32, (sb, 2 * sb), 0)
    kj = lax.broadcasted_iota(jnp.int32, (sb, 2 * sb), 1)
    dist = sb + qi - kj
    in_band = (dist >= 0) & (dist <= sb)
    first_ok = in_band & (kj >= jnp.where(n == 0, sb, 0))
    distf = (dil * dist).astype(F32)
    bias_rest = [jnp.where(in_band, -slopes[h] * distf, NEG_BIG) for h in range(HEADS)]
    bias_first = [jnp.where(first_ok, -slopes[h] * distf, NEG_BIG) for h in range(HEADS)]
    lane_head = lax.broadcasted_iota(jnp.int32, (sb, GROUP_W), 1) // HEAD_DIM

    for j in range(nsub):
        q = q_ref[j * sb:(j + 1) * sb, :].astype(F32) * (HEAD_DIM ** -0.5)
        kk = kbuf[j * sb:(j + 2) * sb, :]
        vv = vbuf[j * sb:(j + 2) * sb, :]
        qs = jnp.concatenate([jnp.where(lane_head == h, q, 0.0) for h in range(HEADS)], axis=0)
        s = _bdot_nt(qs, kk)
        bias = bias_first if j == 0 else bias_rest
        probs, lses = [], []
        for h in range(HEADS):
            sh = s[h * sb:(h + 1) * sb, :] + bias[h]
            m = jnp.max(sh, axis=-1, keepdims=True)
            p = jnp.exp(sh - m)
            l = jnp.sum(p, axis=-1, keepdims=True)
            probs.append((p * (1.0 / l)).astype(BF16))
            lses.append(m + jnp.log(l))
        pv = jnp.dot(jnp.concatenate(probs, axis=0), vv, preferred_element_type=F32)
        o = jnp.zeros((sb, GROUP_W), F32)
        lse = jnp.zeros((sb, GROUP_W), F32)
        for h in range(HEADS):
            o = jnp.where(lane_head == h, pv[h * sb:(h + 1) * sb, :], o)
            lse = jnp.where(lane_head == h, lses[h], lse)
        o_ref[j * sb:(j + 1) * sb, :] = o.astype(o_ref.dtype)
        lse_ref[j * sb:(j + 1) * sb, :] = lse


def _attn_prompt_group(att_g, group):
    bsz, dil, n_sub_rows, _ = att_g.shape
    sb = ATT_STEPS
    nsub = min(ATT_SUB_BLOCKS, n_sub_rows // sb)
    qb = nsub * sb
    assert n_sub_rows % qb == 0 and dil == DIL_GROUPS[group][1]
    cur = lambda sect: pl.BlockSpec((None, None, qb, GROUP_W), lambda b, r, n: (b, r, n, sect))
    prev = lambda sect: pl.BlockSpec((None, None, sb, GROUP_W),
                                     lambda b, r, n: (b, r, jnp.maximum(n * nsub - 1, 0), sect))
    out_spec = pl.BlockSpec((None, None, qb, GROUP_W), lambda b, r, n: (b, r, n, 0))
    return pl.pallas_call(
        functools.partial(_attn_kernel, dil=dil, slopes=_alibi_slopes(group), nsub=nsub),
        grid=(bsz, dil, n_sub_rows // qb),
        in_specs=[cur(0), cur(1), prev(1), cur(2), prev(2)],
        out_specs=[out_spec, out_spec],
        out_shape=[jax.ShapeDtypeStruct((bsz, dil, n_sub_rows, GROUP_W), BF16),
                   jax.ShapeDtypeStruct((bsz, dil, n_sub_rows, GROUP_W), F32)],
        scratch_shapes=[pltpu.VMEM((qb + sb, GROUP_W), att_g.dtype), pltpu.VMEM((qb + sb, GROUP_W), att_g.dtype)],
        compiler_params=_cparams("arbitrary", "arbitrary", "arbitrary"),
        name=f"attn_prompt_g{group}",
    )(att_g, att_g, att_g, att_g, att_g)


def _kv_tail_kernel(a_ref, out_ref, slab, *, dil):
    if dil == 1:
        rows = a_ref[0].astype(F32)
    else:
        rows = _interleave_rows(a_ref, slab, dil)
    out_ref[...] = rows.T


def _kv_tail(att_g, group):
    bsz, dil, n_sub_rows, _ = att_g.shape
    sb = ATT_STEPS
    keep = sb * dil
    assert n_sub_rows % sb == 0, "prompt shorter than the window is not supported"
    return pl.pallas_call(
        functools.partial(_kv_tail_kernel, dil=dil),
        grid=(bsz, 2),
        in_specs=[pl.BlockSpec((None, dil, sb, GROUP_W), lambda b, kv: (b, 0, n_sub_rows // sb - 1, 1 + kv))],
        out_specs=pl.BlockSpec((None, None, GROUP_W, keep), lambda b, kv: (b, kv, 0, 0)),
        out_shape=jax.ShapeDtypeStruct((bsz, 2, GROUP_W, keep), F32),
        scratch_shapes=[pltpu.VMEM((GROUP_W // LANES, keep, LANES), F32)],
        compiler_params=_cparams("arbitrary", "arbitrary"),
        name=f"kv_tail_g{group}",
    )(att_g)


def _dn_gates(ba, alog, dtb):
    beta = _sigmoid(ba)
    g = -jnp.exp(alog) * _softplus(ba + dtb)
    return beta, g


def _l2norm(x):
    return x * lax.rsqrt(jnp.sum(x * x, axis=-1, keepdims=True) + RMS_EPS)


def _gated_rmsnorm(o, z, w):
    return o * lax.rsqrt(jnp.mean(o * o, axis=-1, keepdims=True) + RMS_EPS) * w * _silu(z)


def _dn_prompt_kernel(dn_ref, ba_ref, z_ref, wconv_ref, alog_ref, dtb_ref, onw_ref,
                      o_ref, cst_ref, sfin_ref, xbuf, s_scr, *, nb, same_block=None):
    c = pl.program_id(0)
    ch = DN_CHUNK
    halo = SUBLANES

    @pl.when(c == 0)
    def _():
        xbuf[...] = jnp.zeros_like(xbuf)
        s_scr[...] = jnp.zeros_like(s_scr)

    row = lax.broadcasted_iota(jnp.int32, (ch, ch), 0)
    col = lax.broadcasted_iota(jnp.int32, (ch, ch), 1)
    lower = row >= col
    strict = row > col
    row_l = lax.broadcasted_iota(jnp.int32, (ch, LANES), 0)
    w = wconv_ref[...]
    onw = onw_ref[...]

    acts, betas, gcs, gcts, egcs, eglasts, ekds = [], [], [], [], [], [], []
    for b in range(nb):
        x = dn_ref[b].astype(F32)
        xbuf[b, halo:halo + ch, :] = x
        conv = w[DN_CONV - 1:DN_CONV, :] * x
        for j in range(DN_CONV - 1):
            conv = conv + w[j:j + 1, :] * xbuf[b, halo - (DN_CONV - 1) + j:halo - (DN_CONV - 1) + j + ch, :]
        tail = xbuf[b, halo + ch - (DN_CONV - 1):halo + ch, :]
        cst_ref[b] = tail
        xbuf[b, halo - (DN_CONV - 1):halo, :] = tail
        acts.append(_silu(conv))
        beta_all, g_all = _dn_gates(ba_ref[b], alog_ref[...], dtb_ref[...])
        gc = g_all
        for sft in (1, 2, 4, 8, 16, 32):
            gc = gc + jnp.where(row_l >= sft, pltpu.roll(gc, sft, 0), 0.0)
        glast = gc[ch - 1:ch, :]
        betas.append(beta_all)
        gcs.append(gc)
        gcts.append(gc.T)
        egcs.append(jnp.exp(gc))
        eglasts.append(jnp.exp(glast))
        ekds.append(jnp.exp(glast - gc))

    pairs = [(b, h) for b in range(nb) for h in range(DN_HEADS)]
    ga = lambda h: DN_HEADS + h
    head = lambda a, sect, h: a[:, sect * DN_W + h * DN_DIM:sect * DN_W + (h + 1) * DN_DIM]
    q = [_l2norm(head(acts[b], 0, h)) * (DN_DIM ** -0.5) for b, h in pairs]
    k = [_l2norm(head(acts[b], 1, h)) for b, h in pairs]
    v = [head(acts[b], 2, h) for b, h in pairs]
    beta = [betas[b][:, h:h + 1] for b, h in pairs]
    eg = [egcs[b][:, ga(h):ga(h) + 1] for b, h in pairs]
    decay = [jnp.where(lower, jnp.exp(jnp.minimum(gcs[b][:, ga(h):ga(h) + 1] - gcts[b][ga(h):ga(h) + 1, :], 0.0)), 0.0)
             for b, h in pairs]
    kb = [ki * bi for ki, bi in zip(k, beta)]
    kq = [_bdot_nt(jnp.concatenate([kbi, qi], axis=0), ki) for kbi, qi, ki in zip(kb, q, k)]
    xm = [jnp.where(strict, -(kqi[:ch] * di), 0.0) for kqi, di in zip(kq, decay)]
    attn = [kqi[ch:] * di for kqi, di in zip(kq, decay)]
    sol = [jnp.concatenate([vi * bi, kbi * egi], axis=-1) for vi, bi, kbi, egi in zip(v, beta, kb, eg)]
    n_levels = int(math.log2(ch))
    for lvl in range(n_levels):
        if lvl < n_levels - 1:
            prod = [_bdot(xi, jnp.concatenate([si, xi], axis=-1)) for xi, si in zip(xm, sol)]
            sol = [si + pi[:, :2 * DN_DIM] for si, pi in zip(sol, prod)]
            xm = [pi[:, 2 * DN_DIM:] for pi in prod]
        else:
            sol = [si + _bdot(xi, si) for xi, si in zip(xm, sol)]
    s_old = [s_scr[b, h] for b, h in pairs]
    ws = [_bdot(jnp.concatenate([si[:, DN_DIM:], qi * egi], axis=0), so)
          for si, qi, egi, so in zip(sol, q, eg, s_old)]
    u = [si[:, :DN_DIM] - wi[:ch] for si, wi in zip(sol, ws)]
    kdt = [(ki * ekds[b][:, ga(h):ga(h) + 1]).T for ki, (b, h) in zip(k, pairs)]
    au = [_bdot(jnp.concatenate([ai, kdi], axis=0), ui) for ai, kdi, ui in zip(attn, kdt, u)]
    for i, (b, h) in enumerate(pairs):
        s_scr[b, h] = s_old[i] * eglasts[b][:, ga(h):ga(h) + 1] + au[i][ch:]
        o = ws[i][ch:] + au[i][:ch]
        zf = z_ref[b, :, h * DN_DIM:(h + 1) * DN_DIM].astype(F32)
        o_ref[b, :, h * DN_DIM:(h + 1) * DN_DIM] = _gated_rmsnorm(o, zf, onw).astype(o_ref.dtype)

    if same_block is not None:
        same_block()

    @pl.when(c == pl.num_programs(0) - 1)
    def _():
        sfin_ref[...] = s_scr[...]


def _dn_prompt_parts(dn, ba, z, wconv, alog128, dtb128, onw, bsz, t):
    ch = DN_CHUNK
    blk = lambda w: pl.BlockSpec((bsz, ch, w), lambda c: (0, c, 0))
    return dict(
        steps=t // ch,
        ins=(dn.reshape(bsz, t, DN_QKV_W), ba.reshape(bsz, t, LANES), z.reshape(bsz, t, DN_W),
             wconv, alog128, dtb128, onw),
        in_specs=[blk(DN_QKV_W), blk(LANES), blk(DN_W),
                  _resident(wconv.shape), _resident(alog128.shape), _resident(dtb128.shape), _resident(onw.shape)],
        out_specs=[blk(DN_W),
                   pl.BlockSpec((bsz, DN_CONV - 1, DN_QKV_W), lambda c: (0, 0, 0)),
                   pl.BlockSpec((bsz, DN_HEADS, DN_DIM, DN_DIM), lambda c: (0, 0, 0, 0))],
        out_shape=[jax.ShapeDtypeStruct((bsz, t, DN_W), BF16),
                   jax.ShapeDtypeStruct((bsz, DN_CONV - 1, DN_QKV_W), F32),
                   jax.ShapeDtypeStruct((bsz, DN_HEADS, DN_DIM, DN_DIM), F32)],
        scratch=[pltpu.VMEM((bsz, SUBLANES + ch, DN_QKV_W), F32),
                 pltpu.VMEM((bsz, DN_HEADS, DN_DIM, DN_DIM), F32)])


def _call_parts(kernel_fn, name, *parts):
    steps = parts[0]["steps"]
    assert all(p["steps"] == steps for p in parts)
    cat = lambda key: [x for p in parts for x in p[key]]
    return pl.pallas_call(
        kernel_fn,
        grid=(steps,),
        in_specs=cat("in_specs"),
        out_specs=cat("out_specs"),
        out_shape=cat("out_shape"),
        scratch_shapes=cat("scratch"),
        compiler_params=_cparams("arbitrary"),
        name=name,
    )(*cat("ins"))


def _merge_tail(o_att, odn_ref, gate_ref, x_ref, wa_ref, wd_ref, wo_ref, g_ref, b_ref, h_ref, alpha):
    a = jnp.dot(o_att.astype(BF16), wa_ref[...], preferred_element_type=F32)
    bm = jnp.dot(odn_ref[...].astype(BF16), wd_ref[...], preferred_element_type=F32)
    sg = _sigmoid(gate_ref[...].astype(F32))
    merged = sg[:, :D_MODEL] * a + sg[:, D_MODEL:] * bm
    mix = jnp.dot(merged.astype(BF16), wo_ref[...], preferred_element_type=F32)
    h_ref[...] = _layer_norm(alpha * x_ref[...] + mix, g_ref[...], b_ref[...])


def _merge_prompt_kernel(o0_ref, l0_ref, o1_ref, l1_ref, o2_ref, l2_ref, odn_ref, gate_ref, x_ref,
                         wa_ref, wd_ref, wo_ref, g_ref, b_ref, h_ref, slabs, *, alpha):
    outs, lses = [], []
    for gi, (o_ref, l_ref) in enumerate(((o0_ref, l0_ref), (o1_ref, l1_ref), (o2_ref, l2_ref))):
        dil = DIL_GROUPS[gi][1]
        if dil == 1:
            outs.append(o_ref[0].astype(F32))
            lses.append(l_ref[0])
        else:
            outs.append(_interleave_rows(o_ref, slabs.at[2 * (gi - 1)], dil))
            lses.append(_interleave_rows(l_ref, slabs.at[2 * (gi - 1) + 1], dil))
    m = functools.reduce(jnp.maximum, lses)
    es = [jnp.exp(l - m) for l in lses]
    o_att = sum(e * o for e, o in zip(es, outs)) * (1.0 / sum(es))
    _merge_tail(o_att, odn_ref, gate_ref, x_ref, wa_ref, wd_ref, wo_ref, g_ref, b_ref, h_ref, alpha)


def _merge_prompt(att_parts, odn, gates, x2d, wa, wd, wo, g, b, bsz, t, alpha):
    tm = ROW_TILE
    nt = t // tm
    row = lambda w: pl.BlockSpec((tm, w), lambda i: (i, 0))
    att_specs = []
    for _, dil in DIL_GROUPS:
        att_specs += [pl.BlockSpec((None, dil, tm // dil, GROUP_W), lambda i: (i // nt, 0, i % nt, 0))] * 2
    return pl.pallas_call(
        functools.partial(_merge_prompt_kernel, alpha=alpha),
        grid=(bsz * nt,),
        in_specs=att_specs + [row(DN_W), row(2 * D_MODEL), row(D_MODEL),
                              _resident(wa.shape), _resident(wd.shape), _resident(wo.shape),
                              _resident(g.shape), _resident(b.shape)],
        out_specs=row(D_MODEL),
        out_shape=jax.ShapeDtypeStruct((bsz * t, D_MODEL), F32),
        scratch_shapes=[pltpu.VMEM((2 * (N_GROUPS - 1), GROUP_W // LANES, tm, LANES), F32)],
        compiler_params=_cparams("arbitrary"),
        name="merge_ln1_prompt",
    )(*att_parts, odn, gates, x2d, wa, wd, wo, g, b)


def _merge_rows_kernel(oatt_ref, odn_ref, gate_ref, x_ref, wa_ref, wd_ref, wo_ref, g_ref, b_ref, h_ref, *, alpha):
    _merge_tail(oatt_ref[...], odn_ref, gate_ref, x_ref, wa_ref, wd_ref, wo_ref, g_ref, b_ref, h_ref, alpha)


def _merge_rows(oatt, odn, gates, x2d, wa, wd, wo, g, b, alpha):
    m = x2d.shape[0]
    ins = (oatt, odn, gates, x2d, wa, wd, wo, g, b)
    return pl.pallas_call(
        functools.partial(_merge_rows_kernel, alpha=alpha),
        grid=(1,),
        in_specs=[pl.BlockSpec(a.shape, lambda i: (0, 0)) for a in ins],
        out_specs=pl.BlockSpec((m, D_MODEL), lambda i: (0, 0)),
        out_shape=jax.ShapeDtypeStruct((m, D_MODEL), F32),
        compiler_params=_cparams("arbitrary"),
        name="merge_ln1_rows",
    )(*ins)


def _ffn_prompt_kernel(h_ref, wup_ref, wconv_ref, bconv_ref, wdn_ref, g_ref, b_ref,
                       y_ref, cst_ref, halo, *, alpha):
    @pl.when(pl.program_id(1) == 0)
    def _():
        halo[...] = jnp.zeros_like(halo)

    h = h_ref[...]
    tm = h.shape[0]
    hb = h.astype(BF16)
    starts = [sum(FF_CHUNKS[:i]) for i in range(len(FF_CHUNKS))]

    def up(ci):
        lo, width = starts[ci], FF_CHUNKS[ci]
        return (jnp.dot(hb, wup_ref[:, lo:lo + width], preferred_element_type=F32),
                jnp.dot(hb, wup_ref[:, D_FF + lo:D_FF + lo + width], preferred_element_type=F32))

    acc = jnp.zeros((tm, D_MODEL), F32)
    nxt = up(0)
    for ci, (lo, width) in enumerate(zip(starts, FF_CHUNKS)):
        cols = slice(lo, lo + width)
        a, gt = nxt
        if ci + 1 < len(FF_CHUNKS):
            nxt = up(ci + 1)
        prev2 = halo[0:1, cols]
        prev1 = halo[1:2, cols]
        r1 = pltpu.roll(a, 1, 0)
        r2 = pltpu.roll(a, 2, 0)
        row = lax.broadcasted_iota(jnp.int32, (SUBLANES, width), 0)
        a1 = jnp.concatenate([jnp.where(row == 0, prev1, r1[:SUBLANES]), r1[SUBLANES:]], axis=0)
        a2 = jnp.concatenate([jnp.where(row == 0, prev2, jnp.where(row == 1, prev1, r2[:SUBLANES])), r2[SUBLANES:]],
                             axis=0)
        wc = wconv_ref[:, cols]
        conv = wc[0:1, :] * a2 + wc[1:2, :] * a1 + wc[2:3, :] * a + bconv_ref[:, cols]
        act = _gelu_tanh(conv) * gt
        acc = acc + jnp.dot(act.astype(BF16), wdn_ref[cols, :], preferred_element_type=F32)
        tail = a[tm - (FFN_CONV - 1):tm, :]
        halo[0:FFN_CONV - 1, cols] = tail
        cst_ref[:, cols] = tail
    y_ref[...] = _layer_norm(alpha * h + acc, g_ref[...], b_ref[...])


def _ffn_prompt(h2d, wup, wconv, bconv, wdn, g, b, bsz, t, alpha):
    tm = FFN_ROW_TILE
    assert sum(FF_CHUNKS) == D_FF and t % tm == 0
    nt = t // tm
    return pl.pallas_call(
        functools.partial(_ffn_prompt_kernel, alpha=alpha),
        grid=(bsz, nt),
        in_specs=[pl.BlockSpec((tm, D_MODEL), lambda bi, i: (bi * nt + i, 0)),
                  _resident(wup.shape), _resident(wconv.shape), _resident(bconv.shape), _resident(wdn.shape),
                  _resident(g.shape), _resident(b.shape)],
        out_specs=[pl.BlockSpec((tm, D_MODEL), lambda bi, i: (bi * nt + i, 0)),
                   pl.BlockSpec((None, FFN_CONV - 1, D_FF), lambda bi, i: (bi, 0, 0))],
        out_shape=[jax.ShapeDtypeStruct((bsz * t, D_MODEL), F32),
                   jax.ShapeDtypeStruct((bsz, FFN_CONV - 1, D_FF), F32)],
        scratch_shapes=[pltpu.VMEM((SUBLANES, D_FF), F32)],
        compiler_params=_cparams("arbitrary", "arbitrary"),
        name="convffn_prompt",
    )(h2d, wup, wconv, bconv, wdn, g, b)


def _ffn_step_kernel(h_ref, prev_ref, wup_ref, wconv_ref, bconv_ref, wdn_ref, g_ref, b_ref,
                     y_ref, st_ref, *, alpha):
    h = h_ref[...]
    hb = h.astype(BF16)
    a = jnp.dot(hb, wup_ref[:, :D_FF], preferred_element_type=F32)
    gt = jnp.dot(hb, wup_ref[:, D_FF:], preferred_element_type=F32)
    p0 = prev_ref[:, :D_FF]
    p1 = prev_ref[:, D_FF:]
    wc = wconv_ref[...]
    conv = wc[0:1, :] * p0 + wc[1:2, :] * p1 + wc[2:3, :] * a + bconv_ref[...]
    act = _gelu_tanh(conv) * gt
    f = jnp.dot(act.astype(BF16), wdn_ref[...], preferred_element_type=F32)
    y_ref[...] = _layer_norm(alpha * h + f, g_ref[...], b_ref[...])
    st_ref[:, :D_FF] = p1
    st_ref[:, D_FF:] = a


def _ffn_step(h2d, prev2d, wup, wconv, bconv, wdn, g, b, alpha):
    m = h2d.shape[0]
    full = lambda a: pl.BlockSpec(a.shape, lambda i: (0,) * a.ndim)
    ins = (h2d, prev2d, wup, wconv, bconv, wdn, g, b)
    return pl.pallas_call(
        functools.partial(_ffn_step_kernel, alpha=alpha),
        grid=(1,),
        in_specs=[full(a) for a in ins],
        out_specs=[pl.BlockSpec((m, D_MODEL), lambda i: (0, 0)),
                   pl.BlockSpec((m, (FFN_CONV - 1) * D_FF), lambda i: (0, 0))],
        out_shape=[jax.ShapeDtypeStruct((m, D_MODEL), F32),
                   jax.ShapeDtypeStruct((m, (FFN_CONV - 1) * D_FF), F32)],
        compiler_params=_cparams("arbitrary"),
        name="convffn_step",
    )(*ins)


def _mixer_step_kernel(att_ref, dn_ref, z_ref, ba_ref, c0_ref, c1_ref, c2_ref, cst_ref, s_ref,
                       wconv_ref, alog_ref, dtb_ref, onw_ref,
                       oatt_ref, odn_ref, n0_ref, n1_ref, n2_ref, ncst_ref, ns_ref):
    b = pl.program_id(0)
    me = pl.ds(b, 1)
    att = att_ref[me, :]
    sub = lax.broadcasted_iota(jnp.int32, (SUBLANES, GROUP_W), 0)
    lane_head = lax.broadcasted_iota(jnp.int32, (SUBLANES, GROUP_W), 1) // HEAD_DIM
    head_sel = sub == lane_head

    o_parts, lse_parts = [], []
    for gi, (cache_ref, new_ref) in enumerate(((c0_ref, n0_ref), (c1_ref, n1_ref), (c2_ref, n2_ref))):
        window, dil = DIL_GROUPS[gi]
        n = cache_ref.shape[-1]
        assert n == window == ATT_STEPS * dil, "cache shorter than the window is not supported"
        q = att[:, gi * QKV_W:gi * QKV_W + GROUP_W] * (HEAD_DIM ** -0.5)
        k_new = att[:, gi * QKV_W + GROUP_W:gi * QKV_W + 2 * GROUP_W]
        v_new = att[:, gi * QKV_W + 2 * GROUP_W:(gi + 1) * QKV_W]
        kt = cache_ref[0]
        vt = cache_ref[1]
        qm = jnp.where(head_sel, jnp.broadcast_to(q, (SUBLANES, GROUP_W)), 0.0)
        pos = lax.broadcasted_iota(jnp.int32, (SUBLANES, n), 1)
        sub_n = lax.broadcasted_iota(jnp.int32, (SUBLANES, n), 0)
        slopes = _alibi_slopes(gi)
        slope = jnp.zeros((SUBLANES, n), F32)
        for h in range(HEADS):
            slope = jnp.where(sub_n == h, slopes[h], slope)
        on_grid = (pos & (dil - 1)) == 0
        bias = jnp.where(on_grid, -slope * (n - pos).astype(F32), NEG_BIG)
        s_old = _bdot(qm, kt) + bias
        s_new = jnp.sum(qm * k_new, axis=-1, keepdims=True)
        m = jnp.maximum(jnp.max(s_old, axis=-1, keepdims=True), s_new)
        p_old = jnp.exp(s_old - m)
        p_new = jnp.exp(s_new - m)
        l = jnp.sum(p_old, axis=-1, keepdims=True) + p_new
        inv_l = 1.0 / l
        o8 = _bdot_nt(p_old * inv_l, vt) + (p_new * inv_l) * v_new
        o_parts.append(o8)
        lse_parts.append(m + jnp.log(l))
        lane = lax.broadcasted_iota(jnp.int32, (GROUP_W, n), 1)
        new_ref[0] = jnp.where(lane == n - 1, _row_to_col(k_new), pltpu.roll(kt, n - 1, 1))
        new_ref[1] = jnp.where(lane == n - 1, _row_to_col(v_new), pltpu.roll(vt, n - 1, 1))
    mx = functools.reduce(jnp.maximum, lse_parts)
    es = [jnp.exp(l - mx) for l in lse_parts]
    comb = sum(e * o for e, o in zip(es, o_parts)) * (1.0 / sum(es))
    oatt_ref[me, :] = jnp.sum(jnp.where(head_sel, comb, 0.0), axis=0, keepdims=True)

    x_new = dn_ref[me, :]
    w = wconv_ref[...]
    conv = w[DN_CONV - 1:DN_CONV, :] * x_new
    for j in range(DN_CONV - 1):
        old = cst_ref[j, me, :]
        conv = conv + w[j:j + 1, :] * old
        if j > 0:
            ncst_ref[j - 1, me, :] = old
    ncst_ref[DN_CONV - 2, me, :] = x_new
    act = _silu(conv)
    beta_all, g_all = _dn_gates(ba_ref[me, :], alog_ref[...], dtb_ref[...])
    z = z_ref[me, :]
    outs = []
    for h in range(DN_HEADS):
        q = _l2norm(act[:, h * DN_DIM:(h + 1) * DN_DIM]) * (DN_DIM ** -0.5)
        k = _l2norm(act[:, DN_W + h * DN_DIM:DN_W + (h + 1) * DN_DIM])
        v = act[:, 2 * DN_W + h * DN_DIM:2 * DN_W + (h + 1) * DN_DIM]
        beta = beta_all[:, h:h + 1]
        s = s_ref[h] * jnp.exp(g_all[:, DN_HEADS + h:DN_HEADS + h + 1])
        k_col = _row_to_col(k)
        u = beta * (v - jnp.sum(k_col * s, axis=0, keepdims=True))
        s = s + k_col * u
        ns_ref[h] = s
        o = jnp.sum(_row_to_col(q) * s, axis=0, keepdims=True)
        outs.append(_gated_rmsnorm(o, z[:, h * DN_DIM:(h + 1) * DN_DIM], onw_ref[...]))
    odn_ref[me, :] = jnp.concatenate(outs, axis=-1)


def _mixer_step_parts(att, dn, z, ba, caches_t, cst_t, s0, wconv, alog128, dtb128, onw):
    nb = att.shape[0]
    per_b = lambda a: pl.BlockSpec((None,) + a.shape[1:], lambda b: (b,) + (0,) * (a.ndim - 1))
    whole = lambda a: pl.BlockSpec(a.shape, lambda b: (0,) * a.ndim)
    outs = [jax.ShapeDtypeStruct((nb, GROUP_W), F32), jax.ShapeDtypeStruct((nb, DN_W), F32),
            *[jax.ShapeDtypeStruct(c.shape, F32) for c in caches_t],
            jax.ShapeDtypeStruct(cst_t.shape, F32), jax.ShapeDtypeStruct(s0.shape, F32)]
    return dict(
        steps=nb,
        ins=(att, dn, z, ba, *caches_t, cst_t, s0, wconv, alog128, dtb128, onw),
        in_specs=([whole(a) for a in (att, dn, z, ba)] + [per_b(c) for c in caches_t]
                  + [whole(cst_t), per_b(s0)] + [_resident(a.shape) for a in (wconv, alog128, dtb128, onw)]),
        out_specs=[whole(outs[0]), whole(outs[1])] + [per_b(c) for c in caches_t] + [whole(cst_t), per_b(s0)],
        out_shape=outs,
        scratch=[])


N_DN_IN, N_DN_OUT, N_MIX_IN, N_MIX_OUT = 7, 3, 13, 7


def _dn_mixer_kernel(*refs, nb):
    dn_in, refs = refs[:N_DN_IN], refs[N_DN_IN:]
    mix_in, refs = refs[:N_MIX_IN], refs[N_MIX_IN:]
    dn_out, refs = refs[:N_DN_OUT], refs[N_DN_OUT:]
    mix_out, scratch = refs[:N_MIX_OUT], refs[N_MIX_OUT:]
    _dn_prompt_kernel(*dn_in, *dn_out, *scratch, nb=nb,
                      same_block=functools.partial(_mixer_step_kernel, *mix_in, *mix_out))


def _dn_prompt_and_mixer_step(dn_parts, mix_parts, bsz):
    if dn_parts["steps"] == mix_parts["steps"]:
        outs = _call_parts(functools.partial(_dn_mixer_kernel, nb=bsz), "deltanet_prompt_mixer_step",
                           dn_parts, mix_parts)
        return outs[:N_DN_OUT], outs[N_DN_OUT:]
    return (_call_parts(functools.partial(_dn_prompt_kernel, nb=bsz), "deltanet_prompt", dn_parts),
            _call_parts(_mixer_step_kernel, "mixer_step", mix_parts))


def _pad_lanes(v, offset):
    return jnp.zeros((1, LANES), F32).at[0, offset:offset + v.shape[0]].set(v.astype(F32))


def _main_weight(w_in_l):
    att = w_in_l[:, :ATT_QKV_W].reshape(D_MODEL, 3, N_GROUPS, GROUP_W).transpose(0, 2, 1, 3).reshape(D_MODEL, ATT_QKV_W)
    return jnp.concatenate([att, w_in_l[:, ATT_QKV_W:W_MAIN]], axis=1).astype(BF16)


def kernel(x_prompt, x_sample, cache_kv_w128, cache_kv_w512, cache_kv_w2048, state_dn_conv, state_dn_S,
           state_ffn_conv, w_in, w_dn_conv, dn_a_log, dn_dt_bias, dn_onorm_w, w_att_out, w_dn_out, w_o,
           ln1_g, ln1_b, w_up, w_ffn_conv, b_ffn_conv, w_down, ln2_g, ln2_b):
    depth = w_in.shape[0]
    alpha = (2 * depth) ** 0.25
    bsz, t, _ = x_prompt.shape
    nb = x_sample.shape[0]
    assert x_sample.shape[1] == 1 and t % ROW_TILE == 0
    caches_in = (cache_kv_w128, cache_kv_w512, cache_kv_w2048)

    yp = x_prompt.reshape(bsz * t, D_MODEL)
    ys = x_sample.reshape(nb, D_MODEL)
    p_states, s_states = [], []
    for l in range(depth):
        wm = _main_weight(w_in[l])
        wg = w_in[l, :, OFF_GATE:].astype(BF16)
        wa, wd, wo = w_att_out[l].astype(BF16), w_dn_out[l].astype(BF16), w_o[l].astype(BF16)
        wup, wdn = w_up[l].astype(BF16), w_down[l].astype(BF16)
        wconv = w_dn_conv[l]
        alog128 = _pad_lanes(dn_a_log[l], DN_HEADS)
        dtb128 = _pad_lanes(dn_dt_bias[l], DN_HEADS)
        onw = dn_onorm_w[l].reshape(1, DN_DIM)
        g1, b1 = ln1_g[l].reshape(1, D_MODEL), ln1_b[l].reshape(1, D_MODEL)
        g2, b2 = ln2_g[l].reshape(1, D_MODEL), ln2_b[l].reshape(1, D_MODEL)
        wfc, bfc = w_ffn_conv[l], b_ffn_conv[l].reshape(1, D_FF)

        a0, a1, a2, dn, z, ba, gates = _proj_prompt(yp, wm, wg, bsz, t)
        att_s, dn_s, z_s, ba_s, gates_s = _proj_rows(ys, wm, wg)
        att_parts, p_kv = [], []
        for gi, a_g in enumerate((a0, a1, a2)):
            att_parts.extend(_attn_prompt_group(a_g, gi))
            tail = _kv_tail(a_g, gi)
            p_kv.append(tail.reshape(bsz, 2, HEADS, HEAD_DIM, -1).transpose(0, 4, 1, 2, 3))

        caches_t = [c[l].transpose(0, 2, 3, 4, 1).reshape(nb, 2, GROUP_W, -1) for c in caches_in]
        (odn, p_cst, p_s), (oatt_s, odn_s, n0, n1, n2, n_cst, n_s) = _dn_prompt_and_mixer_step(
            _dn_prompt_parts(dn, ba, z, wconv, alog128, dtb128, onw, bsz, t),
            _mixer_step_parts(att_s, dn_s, z_s, ba_s, caches_t, state_dn_conv[l].transpose(1, 0, 2),
                              state_dn_S[l], wconv, alog128, dtb128, onw),
            bsz)

        hp = _merge_prompt(att_parts, odn.reshape(bsz * t, DN_W), gates, yp, wa, wd, wo, g1, b1, bsz, t, alpha)
        yp_new, p_ffn = _ffn_prompt(hp, wup, wfc, bfc, wdn, g2, b2, bsz, t, alpha)
        p_states.append((*p_kv, p_cst, p_s, p_ffn))

        hs = _merge_rows(oatt_s, odn_s, gates_s, ys, wa, wd, wo, g1, b1, alpha)
        ys_new, s_ffn = _ffn_step(hs, state_ffn_conv[l].reshape(nb, (FFN_CONV - 1) * D_FF),
                                  wup, wfc, bfc, wdn, g2, b2, alpha)
        s_kv = [n.reshape(nb, 2, HEADS, HEAD_DIM, -1).transpose(0, 4, 1, 2, 3) for n in (n0, n1, n2)]
        s_states.append((*s_kv, n_cst.transpose(1, 0, 2), n_s, s_ffn.reshape(nb, FFN_CONV - 1, D_FF)))
        yp, ys = yp_new, ys_new

    stacked = lambda states, i: jnp.stack([st[i] for st in states], axis=0)
    return (yp.reshape(bsz, t, D_MODEL), ys.reshape(nb, 1, D_MODEL),
            *[stacked(p_states, i) for i in range(6)],
            *[stacked(s_states, i) for i in range(6)])
```
